```python
import math
import jax
import jax.numpy as jnp
from jax import lax
import numpy as np

D_MODEL = 1024
BATCH = 2
SEQ = 8192
DEPTH = 4

D_MIX = D_MODEL
W_GROUP = D_MIX // 4
D_FF = 2816
NORM_EPS = 1e-6
GROUP_NORM_EPS = 1e-5
CONV_W = 4

GLA_HEADS = 4
GLA_DK = W_GROUP // 2
GLA_DV = W_GROUP
GLA_HK = GLA_DK // GLA_HEADS
GLA_HV = GLA_DV // GLA_HEADS
GLA_RANK = 16
GLA_GATE_NORM = 16.0
GLA_CHUNK = 64
GLA_COLS = 2 * GLA_DK + 2 * GLA_DV + GLA_RANK

LRU_WIDTH = W_GROUP
LRU_BLOCKS = 4
LRU_BS = LRU_WIDTH // LRU_BLOCKS
LRU_C = 8.0
LRU_COLS = 2 * LRU_WIDTH

RW_WIDTH = W_GROUP
RW_HEADS = 4
RW_HS = RW_WIDTH // RW_HEADS
RW_W_RANK = 16
RW_A_RANK = 16
RW_V_RANK = 8
RW_G_RANK = 32
RW_DECAY_SCALE = math.exp(-0.5)
RW_GN_EPS = 64e-5
RW_COLS = 3 * RW_WIDTH + RW_W_RANK + RW_A_RANK + RW_G_RANK

SSD_DINNER = W_GROUP
SSD_HEADDIM = 64
SSD_HEADS = SSD_DINNER // SSD_HEADDIM
SSD_GROUPS = 2
SSD_DSTATE = 128
SSD_CHUNK = 128
SSD_CONV_DIM = SSD_DINNER + 2 * SSD_GROUPS * SSD_DSTATE
SSD_COLS = SSD_DINNER + SSD_CONV_DIM + SSD_HEADS

N_IN = GLA_COLS + LRU_COLS + RW_COLS + SSD_COLS

kernel_name = "hybrid_parallel_groups_gla_rglru_rwkv7_ssd_macaron"


def split_cols(p, widths):
    offsets = [int(o) for o in np.cumsum(widths)[:-1]]
    return jnp.split(p, offsets, axis=-1)


def rmsnorm(x, w, eps=NORM_EPS):
    xf = x.astype(jnp.float32)
    y = xf * lax.rsqrt(jnp.mean(xf * xf, axis=-1, keepdims=True) + eps)
    return (y * w.astype(jnp.float32)).astype(x.dtype)


def swiglu(x, w_gate, w_up, w_down):
    return (jax.nn.silu(x @ w_gate) * (x @ w_up)) @ w_down


def causal_conv(x, w, b):
    k_w, ch = w.shape
    y = lax.conv_general_dilated(
        x, w[:, None, :].astype(x.dtype), window_strides=(1,), padding=[(k_w - 1, 0)],
        dimension_numbers=("NWC", "WIO", "NWC"), feature_group_count=ch)
    return y + b.astype(x.dtype)


def token_shift(x):
    return jnp.pad(x[:, :-1], ((0, 0), (1, 0), (0, 0)))


def gla_chunked(q, k, v, log_a):
    bsz, seq, heads, dk = q.shape
    dv = v.shape[-1]
    n_chunks = seq // GLA_CHUNK

    def chunks(t):
        return t.astype(jnp.float32).reshape(bsz, n_chunks, GLA_CHUNK, heads, t.shape[-1])

    q, k, v, log_a = chunks(q), chunks(k), chunks(v), chunks(log_a)
    b = jnp.cumsum(log_a, axis=2)
    b_last = b[:, :, -1:]
    q_dec = q * jnp.exp(b)
    k_dec = k * jnp.exp(-b)
    causal = jnp.tril(jnp.ones((GLA_CHUNK, GLA_CHUNK), dtype=bool))
    scores = jnp.where(causal, jnp.einsum("bnihd,bnjhd->bnhij", q_dec, k_dec), 0.0)
    o_intra = jnp.einsum("bnhij,bnjhv->bnihv", scores, v)
    kv_chunk = jnp.einsum("bnjhd,bnjhv->nbhdv", k * jnp.exp(b_last - b), v)
    decay_chunk = jnp.exp(jnp.moveaxis(b_last[:, :, 0], 1, 0))

    def step(state, inp):
        dec, kv = inp
        return dec[..., None] * state + kv, state

    _, s_prev = lax.scan(step, jnp.zeros((bsz, heads, dk, dv), jnp.float32), (decay_chunk, kv_chunk))
    o_inter = jnp.einsum("bnihd,nbhdv->bnihv", q_dec, s_prev)
    return (o_intra + o_inter).reshape(bsz, seq, heads, dv)


def gla_group(p, alpha_up, alpha_bias, norm_w):
    bsz, seq, _ = p.shape
    q, k, v, g, stem = split_cols(p, (GLA_DK, GLA_DK, GLA_DV, GLA_DV, GLA_RANK))
    log_a = jax.nn.log_sigmoid((stem @ alpha_up + alpha_bias).astype(jnp.float32)) / GLA_GATE_NORM
    heads_k = lambda t: t.reshape(bsz, seq, GLA_HEADS, GLA_HK)
    o = gla_chunked(heads_k(q) * (GLA_HK ** -0.5), heads_k(k),
                    v.reshape(bsz, seq, GLA_HEADS, GLA_HV), heads_k(log_a))
    o = rmsnorm(o, norm_w, GROUP_NORM_EPS).reshape(bsz, seq, GLA_DV).astype(p.dtype)
    return o * jax.nn.silu(g)


def _linear_recurrence_combine(left, right):
    a_l, b_l = left
    a_r, b_r = right
    return a_l * a_r, a_r * b_l + b_r


def rglru_group(p, conv_w, conv_b, w_a, b_a, w_x, b_x, lam):
    bsz, seq, _ = p.shape
    xb, gate = split_cols(p, (LRU_WIDTH, LRU_WIDTH))
    xb = causal_conv(xb, conv_w, conv_b)
    xblk = xb.reshape(bsz, seq, LRU_BLOCKS, LRU_BS)
    r = jax.nn.sigmoid(jnp.einsum("bsnk,nkj->bsnj", xblk, w_a).reshape(bsz, seq, LRU_WIDTH) + b_a)
    i = jax.nn.sigmoid(jnp.einsum("bsnk,nkj->bsnj", xblk, w_x).reshape(bsz, seq, LRU_WIDTH) + b_x)
    log_a = -LRU_C * r.astype(jnp.float32) * jax.nn.softplus(-lam.astype(jnp.float32))
    a = jnp.exp(log_a)
    u = jnp.sqrt(-jnp.expm1(2.0 * log_a)) * (i * xb).astype(jnp.float32)
    _, h = lax.associative_scan(_linear_recurrence_combine, (a, u), axis=1)
    return h.astype(p.dtype) * jax.nn.gelu(gate)


def rwkv7_scan(r, w, k, v, a, b):
    bsz, _, heads, n = r.shape
    xs = tuple(jnp.moveaxis(t, 1, 0) for t in (r, w, k, v, a, b))

    def step(state, inp):
        r_t, w_t, k_t, v_t, a_t, b_t = inp
        sa = jnp.einsum("bhvk,bhk->bhv", state, a_t)
        state = (state * w_t[:, :, None, :] + sa[..., None] * b_t[:, :, None, :]
                 + v_t[..., None] * k_t[:, :, None, :])
        return state, jnp.einsum("bhvk,bhk->bhv", state, r_t)

    _, y = lax.scan(step, jnp.zeros((bsz, heads, n, n), jnp.float32), xs)
    return jnp.moveaxis(y, 0, 1)


def rwkv7_group(p, mu, w0, w2, a0, a2, g2, k_k, k_a, r_k, gn_w, gn_b, v_first, v_mix):
    bsz, seq, _ = p.shape
    p = p + (token_shift(p) - p) * mu
    r, k, v, s_w, s_a, s_g = split_cols(p, (RW_WIDTH, RW_WIDTH, RW_WIDTH, RW_W_RANK, RW_A_RANK, RW_G_RANK))
    log_w = -RW_DECAY_SCALE * jax.nn.sigmoid((w0 + jnp.tanh(s_w) @ w2).astype(jnp.float32))
    a = jax.nn.sigmoid(a0 + s_a @ a2)
    g = jax.nn.sigmoid(s_g) @ g2
    if v_mix is not None:
        v0, v1, v2 = v_mix
        v = v + (v_first - v) * jax.nn.sigmoid(v0 + (v @ v1) @ v2)
    heads = lambda t: t.astype(jnp.float32).reshape(bsz, seq, RW_HEADS, RW_HS)
    kk = heads(k * k_k)
    kk = kk / jnp.maximum(jnp.sqrt(jnp.sum(kk * kk, axis=-1, keepdims=True)), 1e-12)
    k = k * (1.0 + (a - 1.0) * k_a)
    rh, kh, vh, ah = heads(r), heads(k), heads(v), heads(a)
    y = rwkv7_scan(rh, heads(jnp.exp(log_w)), kh, vh, -kk, kk * ah)
    mean = jnp.mean(y, axis=-1, keepdims=True)
    var = jnp.mean(jnp.square(y - mean), axis=-1, keepdims=True)
    y = ((y - mean) * lax.rsqrt(var + RW_GN_EPS)).reshape(bsz, seq, RW_WIDTH) * gn_w + gn_b
    bonus = jnp.sum(rh * kh * r_k.astype(jnp.float32), axis=-1, keepdims=True) * vh
    y = (y + bonus.reshape(bsz, seq, RW_WIDTH)).astype(p.dtype) * g
    return y, v


def ssd_chunked(x, d_a, b_in, c_in):
    bsz, seq, heads, hp = x.shape
    groups, n = b_in.shape[2], b_in.shape[3]
    rep = heads // groups
    nc = seq // SSD_CHUNK
    x = x.reshape(bsz, nc, SSD_CHUNK, groups, rep, hp)
    d_a = d_a.reshape(bsz, nc, SSD_CHUNK, groups, rep)
    b_in = b_in.astype(jnp.float32).reshape(bsz, nc, SSD_CHUNK, groups, n)
    c_in = c_in.astype(jnp.float32).reshape(bsz, nc, SSD_CHUNK, groups, n)
    cs = jnp.cumsum(d_a, axis=2)
    seg = cs[:, :, :, None] - cs[:, :, None, :]
    causal = jnp.tril(jnp.ones((SSD_CHUNK, SSD_CHUNK), dtype=bool))[:, :, None, None]
    decay = jnp.exp(jnp.where(causal, seg, -jnp.inf))
    cb = jnp.einsum("bclgn,bcsgn->bclsg", c_in, b_in)
    y_diag = jnp.einsum("bclsgr,bcsgrp->bclgrp", cb[..., None] * decay, x)
    cs_last = cs[:, :, -1]
    x_to_end = x * jnp.exp(cs_last[:, :, None] - cs)[..., None]
    states = jnp.einsum("bcsgn,bcsgrp->cbgrpn", b_in, x_to_end)

    def step(h, inp):
        dec, st = inp
        return dec[..., None, None] * h + st, h

    _, h_prev = lax.scan(step, jnp.zeros((bsz, groups, rep, hp, n), jnp.float32),
                         (jnp.moveaxis(jnp.exp(cs_last), 1, 0), states))
    y_off = jnp.einsum("bclgn,cbgrpn->bclgrp", c_in, h_prev) * jnp.exp(cs)[..., None]
    return (y_diag + y_off).reshape(bsz, seq, heads, hp)


def mamba2_group(p, conv_w, conv_b, dt_bias, a_log, d_skip, norm_w):
    bsz, seq, _ = p.shape
    z, xbc, dt = split_cols(p, (SSD_DINNER, SSD_CONV_DIM, SSD_HEADS))
    xbc = jax.nn.silu(causal_conv(xbc, conv_w, conv_b))
    xs, b_in, c_in = split_cols(xbc, (SSD_DINNER, SSD_GROUPS * SSD_DSTATE, SSD_GROUPS * SSD_DSTATE))
    dt = jax.nn.softplus(dt.astype(jnp.float32) + dt_bias.astype(jnp.float32))
    a = -jnp.exp(a_log.astype(jnp.float32))
    xh = xs.astype(jnp.float32).reshape(bsz, seq, SSD_HEADS, SSD_HEADDIM)
    grp = lambda t: t.reshape(bsz, seq, SSD_GROUPS, SSD_DSTATE)
    y = ssd_chunked(xh * dt[..., None], dt * a, grp(b_in), grp(c_in))
    y = y + d_skip.astype(jnp.float32)[:, None] * xh
    y = y.reshape(bsz, seq, SSD_DINNER) * jax.nn.silu(z.astype(jnp.float32))
    gsz = SSD_DINNER // SSD_GROUPS
    y = rmsnorm(y.reshape(bsz, seq, SSD_GROUPS, gsz), norm_w.reshape(SSD_GROUPS, gsz), GROUP_NORM_EPS)
    return y.reshape(bsz, seq, SSD_DINNER).astype(p.dtype)


def setup_inputs(seed: int = 0) -> dict:
    key = jax.random.key(seed)
    keys = jax.random.split(key, 64)
    counter = [0]

    def nk():
        k = keys[counter[0]]
        counter[0] += 1
        return k

    def nrm(shape, scale):
        return jax.random.normal(nk(), shape, jnp.float32) * scale

    def gain(shape):
        return 1.0 + nrm(shape, 0.02)

    def unif(shape, lo, hi):
        return jax.random.uniform(nk(), shape, jnp.float32, lo, hi)

    L = DEPTH
    lru_s = unif((L, LRU_WIDTH), 0.9, 0.999) ** (1.0 / LRU_C)
    dt0 = jnp.exp(unif((L, SSD_HEADS), math.log(1e-3), math.log(1e-1)))
    return {
        "x": nrm((BATCH, SEQ, D_MODEL), 1.0),
        "ffn1_norm": gain((L, D_MODEL)),
        "ffn1_w_gate": nrm((L, D_MODEL, D_FF), D_MODEL ** -0.5),
        "ffn1_w_up": nrm((L, D_MODEL, D_FF), D_MODEL ** -0.5),
        "ffn1_w_down": nrm((L, D_FF, D_MODEL), D_FF ** -0.5),
        "mix_norm": gain((L, D_MODEL)),
        "w_in": nrm((L, D_MODEL, N_IN), D_MODEL ** -0.5),
        "w_out": nrm((L, D_MIX, D_MODEL), D_MIX ** -0.5),
        "gla_alpha_up": nrm((L, GLA_RANK, GLA_DK), GLA_RANK ** -0.5),
        "gla_alpha_bias": nrm((L, GLA_DK), 0.1),
        "gla_norm": gain((L, GLA_HV)),
        "lru_conv_w": nrm((L, CONV_W, LRU_WIDTH), CONV_W ** -0.5),
        "lru_conv_b": nrm((L, LRU_WIDTH), 0.02),
        "lru_w_a": nrm((L, LRU_BLOCKS, LRU_BS, LRU_BS), LRU_BS ** -0.5),
        "lru_b_a": nrm((L, LRU_WIDTH), 0.02),
        "lru_w_x": nrm((L, LRU_BLOCKS, LRU_BS, LRU_BS), LRU_BS ** -0.5),
        "lru_b_x": nrm((L, LRU_WIDTH), 0.02),
        "lru_lambda": jnp.log(lru_s) - jnp.log1p(-lru_s),
        "rw_mu": unif((L, RW_COLS), 0.0, 1.0),
        "rw_w0": nrm((L, RW_WIDTH), 1.0),
        "rw_w2": nrm((L, RW_W_RANK, RW_WIDTH), RW_W_RANK ** -0.5),
        "rw_a0": nrm((L, RW_WIDTH), 0.1),
        "rw_a2": nrm((L, RW_A_RANK, RW_WIDTH), RW_A_RANK ** -0.5),
        "rw_g2": nrm((L, RW_G_RANK, RW_WIDTH), RW_G_RANK ** -0.5),
        "rw_v0": nrm((L - 1, RW_WIDTH), 0.1),
        "rw_v1": nrm((L - 1, RW_WIDTH, RW_V_RANK), RW_WIDTH ** -0.5),
        "rw_v2": nrm((L - 1, RW_V_RANK, RW_WIDTH), RW_V_RANK ** -0.5),
        "rw_k_k": 0.85 + nrm((L, RW_WIDTH), 0.02),
        "rw_k_a": gain((L, RW_WIDTH)),
        "rw_r_k": nrm((L, RW_HEADS, RW_HS), 0.1),
        "rw_gn_w": gain((L, RW_WIDTH)),
        "rw_gn_b": nrm((L, RW_WIDTH), 0.02),
        "ssd_conv_w": nrm((L, CONV_W, SSD_CONV_DIM), CONV_W ** -0.5),
        "ssd_conv_b": nrm((L, SSD_CONV_DIM), 0.02),
        "ssd_dt_bias": dt0 + jnp.log(-jnp.expm1(-dt0)),
        "ssd_a_log": jnp.log(unif((L, SSD_HEADS), 1.0, 16.0)),
        "ssd_d": gain((L, SSD_HEADS)),
        "ssd_norm": gain((L, SSD_DINNER)),
        "ffn2_norm": gain((L, D_MODEL)),
        "ffn2_w_gate": nrm((L, D_MODEL, D_FF), D_MODEL ** -0.5),
        "ffn2_w_up": nrm((L, D_MODEL, D_FF), D_MODEL ** -0.5),
        "ffn2_w_down": nrm((L, D_FF, D_MODEL), D_FF ** -0.5),
        "final_norm": gain((D_MODEL,)),
    }


def reference(x, ffn1_norm, ffn1_w_gate, ffn1_w_up, ffn1_w_down, mix_norm, w_in, w_out,
              gla_alpha_up, gla_alpha_bias, gla_norm,
              lru_conv_w, lru_conv_b, lru_w_a, lru_b_a, lru_w_x, lru_b_x, lru_lambda,
              rw_mu, rw_w0, rw_w2, rw_a0, rw_a2, rw_g2, rw_v0, rw_v1, rw_v2,
              rw_k_k, rw_k_a, rw_r_k, rw_gn_w, rw_gn_b,
              ssd_conv_w, ssd_conv_b, ssd_dt_bias, ssd_a_log, ssd_d, ssd_norm,
              ffn2_norm, ffn2_w_gate, ffn2_w_up, ffn2_w_down, final_norm):
    v_first = None
    for l in range(DEPTH):
        x = x + 0.5 * swiglu(rmsnorm(x, ffn1_norm[l]), ffn1_w_gate[l], ffn1_w_up[l], ffn1_w_down[l])
        proj = rmsnorm(x, mix_norm[l]) @ w_in[l]
        p_gla, p_lru, p_rw, p_ssd = split_cols(proj, (GLA_COLS, LRU_COLS, RW_COLS, SSD_COLS))
        y_gla = gla_group(p_gla, gla_alpha_up[l], gla_alpha_bias[l], gla_norm[l])
        y_lru = rglru_group(p_lru, lru_conv_w[l], lru_conv_b[l], lru_w_a[l], lru_b_a[l],
                            lru_w_x[l], lru_b_x[l], lru_lambda[l])
        v_mix = None if l == 0 else (rw_v0[l - 1], rw_v1[l - 1], rw_v2[l - 1])
        y_rw, v_rw = rwkv7_group(p_rw, rw_mu[l], rw_w0[l], rw_w2[l], rw_a0[l], rw_a2[l], rw_g2[l],
                                 rw_k_k[l], rw_k_a[l], rw_r_k[l], rw_gn_w[l], rw_gn_b[l], v_first, v_mix)
        if l == 0:
            v_first = v_rw
        y_ssd = mamba2_group(p_ssd, ssd_conv_w[l], ssd_conv_b[l], ssd_dt_bias[l], ssd_a_log[l],
                             ssd_d[l], ssd_norm[l])
        y = jnp.concatenate([y_gla, y_lru, y_rw, y_ssd], axis=-1)
        x = x + y @ w_out[l]
        x = x + 0.5 * swiglu(rmsnorm(x, ffn2_norm[l]), ffn2_w_gate[l], ffn2_w_up[l], ffn2_w_down[l])
    return rmsnorm(x, final_norm)
```

```python
import functools
import math

import numpy as np
import jax
import jax.numpy as jnp
from jax import lax
from jax.experimental import pallas as pl
from jax.experimental.pallas import tpu as pltpu

f32 = jnp.float32
bf16 = jnp.bfloat16

D_MODEL = 1024
DEPTH = 4
D_FF = 2816
W_GROUP = 256
NORM_EPS = 1e-6
GROUP_NORM_EPS = 1e-5
CONV_W = 4

GLA_HEADS = 4
GLA_DK = 128
GLA_HK = 32
GLA_HV = 64
GLA_RANK = 16
GLA_GATE_NORM = 16.0
GLA_CHUNK = 64

LRU_C = 8.0

RW_HS = 64
RW_DECAY_SCALE = math.exp(-0.5)
RW_GN_EPS = 64e-5
RW_CHUNK = 64

SSD_HEADS = 4
SSD_DSTATE = 128
SSD_CHUNK = 128

_GLA0, _LRU0, _RW0, _SSD0, _N_IN = 0, 784, 1296, 2128, 3156
SM_GLA = 0
SM_RW = 16
SM_DT = 80
SM_W = 128

LRU_TILE = 256
VMEM_LIMIT = 56 * 1024 * 1024

HI = lax.Precision.HIGHEST


def _dot(a, b):
    return jnp.dot(a.astype(bf16), b.astype(bf16), preferred_element_type=f32)


def _dot_nt(a, b):
    return lax.dot_general(a.astype(bf16), b.astype(bf16), (((1,), (1,)), ((), ())),
                           preferred_element_type=f32)


def _dot_tn(a, b):
    return lax.dot_general(a.astype(bf16), b.astype(bf16), (((0,), (0,)), ((), ())),
                           preferred_element_type=f32)


def _dot_hi(a, b):
    return jnp.dot(a, b, preferred_element_type=f32, precision=HI)


def _seg_dot(x, ones_bf):
    hi = x.astype(bf16)
    lo = (x - hi.astype(f32)).astype(bf16)
    return (jnp.dot(hi, ones_bf, preferred_element_type=f32)
            + jnp.dot(lo, ones_bf, preferred_element_type=f32))


def _cumsum_rows(x):
    n = x.shape[0]
    row = lax.broadcasted_iota(jnp.int32, x.shape, 0)
    s = 1
    while s < n:
        x = x + jnp.where(row >= s, pltpu.roll(x, s, 0), 0.0)
        s *= 2
    return x


def _stack_heads(x, seg):
    lane = lax.broadcasted_iota(jnp.int32, x.shape, 1) // seg
    return jnp.concatenate([jnp.where(lane == h, x, 0.0) for h in range(4)], axis=0)


def _expand_heads(x, lane0, width):
    n = x.shape[0]
    lane = lax.broadcasted_iota(jnp.int32, (n, width), 1) // (width // 4)
    out = jnp.zeros((n, width), f32)
    for h in range(4):
        out = jnp.where(lane == h, x[:, lane0 + h:lane0 + h + 1], out)
    return out


def _ffn_body(x_ref, nw_ref, wg_ref, wu_ref, wd_ref, o_ref, xn_ref, acc_ref, *, nj):
    j = pl.program_id(1)

    @pl.when(j == 0)
    def _():
        x = x_ref[...]
        ms = jnp.mean(x * x, axis=-1, keepdims=True)
        xn_ref[...] = (x * lax.rsqrt(ms + NORM_EPS) * nw_ref[...]).astype(bf16)
        acc_ref[...] = jnp.zeros_like(acc_ref)

    xn = xn_ref[...]
    g = jnp.dot(xn, wg_ref[...], preferred_element_type=f32)
    u = jnp.dot(xn, wu_ref[...], preferred_element_type=f32)
    h = (jax.nn.silu(g) * u).astype(bf16)
    acc_ref[...] += jnp.dot(h, wd_ref[...], preferred_element_type=f32)

    @pl.when(j == nj - 1)
    def _():
        o_ref[...] = x_ref[...] + 0.5 * acc_ref[...]


def _ffn(x2, nw, wg, wu, wd, layer, *, tm, tf):
    t, d = x2.shape
    nj = wg.shape[-1] // tf
    return pl.pallas_call(
        functools.partial(_ffn_body, nj=nj),
        grid=(t // tm, nj),
        in_specs=[
            pl.BlockSpec((tm, d), lambda i, j: (i, 0)),
            pl.BlockSpec((1, d), lambda i, j: (0, 0)),
            pl.BlockSpec((None, d, tf), lambda i, j: (layer, 0, j)),
            pl.BlockSpec((None, d, tf), lambda i, j: (layer, 0, j)),
            pl.BlockSpec((None, tf, d), lambda i, j: (layer, j, 0)),
        ],
        out_specs=pl.BlockSpec((tm, d), lambda i, j: (i, 0)),
        out_shape=jax.ShapeDtypeStruct((t, d), f32),
        scratch_shapes=[pltpu.VMEM((tm, d), bf16), pltpu.VMEM((tm, d), f32)],
        compiler_params=pltpu.CompilerParams(
            dimension_semantics=("parallel", "arbitrary"), vmem_limit_bytes=VMEM_LIMIT),
        name="ffn",
    )(x2, nw, wg, wu, wd)


_SEG_W = (768, 512, 768, 1024, SM_W)
_SEG_O = tuple(int(o) for o in np.cumsum((0,) + _SEG_W))


def _inproj_body(x_ref, nw_ref, w_ref, o_gla, o_lru, o_rw, o_ssd, o_sm):
    x = x_ref[...]
    ms = jnp.mean(x * x, axis=-1, keepdims=True)
    xn = (x * lax.rsqrt(ms + NORM_EPS) * nw_ref[...]).astype(bf16)
    for o_ref, lo, hi in zip((o_gla, o_lru, o_rw, o_ssd, o_sm), _SEG_O[:-1], _SEG_O[1:]):
        o_ref[...] = jnp.dot(xn, w_ref[:, lo:hi], preferred_element_type=f32)


def _inproj(x2, nw, w_in_p, layer, *, tm):
    t, d = x2.shape
    n = w_in_p.shape[-1]
    return pl.pallas_call(
        _inproj_body,
        grid=(t // tm,),
        in_specs=[
            pl.BlockSpec((tm, d), lambda i: (i, 0)),
            pl.BlockSpec((1, d), lambda i: (0, 0)),
            pl.BlockSpec((None, d, n), lambda i: (layer, 0, 0)),
        ],
        out_specs=[pl.BlockSpec((tm, w), lambda i: (i, 0)) for w in _SEG_W],
        out_shape=[jax.ShapeDtypeStruct((t, w), f32) for w in _SEG_W],
        compiler_params=pltpu.CompilerParams(
            dimension_semantics=("parallel",), vmem_limit_bytes=VMEM_LIMIT),
        name="inproj",
    )(x2, nw, w_in_p)


def _outproj_body(x_ref, y0, y1, y2, y3, w_ref, o_ref):
    acc = x_ref[...]
    for i, y in enumerate((y0, y1, y2, y3)):
        acc = acc + jnp.dot(y[...].astype(bf16), w_ref[i * W_GROUP:(i + 1) * W_GROUP, :],
                            preferred_element_type=f32)
    o_ref[...] = acc


def _outproj(x2, ys, w_out, layer, *, tm):
    t, d = x2.shape
    return pl.pallas_call(
        _outproj_body,
        grid=(t // tm,),
        in_specs=[pl.BlockSpec((tm, d), lambda i: (i, 0))]
        + [pl.BlockSpec((tm, W_GROUP), lambda i: (i, 0)) for _ in range(4)]
        + [pl.BlockSpec((None, d, d), lambda i: (layer, 0, 0))],
        out_specs=pl.BlockSpec((tm, d), lambda i: (i, 0)),
        out_shape=jax.ShapeDtypeStruct((t, d), f32),
        compiler_params=pltpu.CompilerParams(
            dimension_semantics=("parallel",), vmem_limit_bytes=VMEM_LIMIT),
        name="outproj",
    )(x2, *ys, w_out)


def _final_norm_body(x_ref, nw_ref, o_ref):
    x = x_ref[...]
    ms = jnp.mean(x * x, axis=-1, keepdims=True)
    o_ref[...] = x * lax.rsqrt(ms + NORM_EPS) * nw_ref[...]


def _final_norm(x2, nw, *, tm):
    t, d = x2.shape
    return pl.pallas_call(
        _final_norm_body,
        grid=(t // tm,),
        in_specs=[pl.BlockSpec((tm, d), lambda i: (i, 0)), pl.BlockSpec((1, d), lambda i: (0, 0))],
        out_specs=pl.BlockSpec((tm, d), lambda i: (i, 0)),
        out_shape=jax.ShapeDtypeStruct((t, d), f32),
        compiler_params=pltpu.CompilerParams(dimension_semantics=("parallel",)),
        name="final_norm",
    )(x2, nw)


def _gla_body(p_ref, sm_ref, wup_ref, bias_ref, nw_ref, ones_ref, o_ref, st_ref, *, nb):
    c = pl.program_id(0)
    L = GLA_CHUNK

    @pl.when(c == 0)
    def _():
        st_ref[...] = jnp.zeros_like(st_ref)

    row = lax.broadcasted_iota(jnp.int32, (L, 4 * L), 0)
    col = lax.broadcasted_iota(jnp.int32, (L, 4 * L), 1)
    causal = (col % L) <= row
    srow = lax.broadcasted_iota(jnp.int32, (W_GROUP, GLA_DK), 0) // GLA_HV
    scol = lax.broadcasted_iota(jnp.int32, (W_GROUP, GLA_DK), 1) // GLA_HK
    blockdiag = srow == scol

    for b in range(nb):
        x = p_ref[b]
        q = x[:, 0:128] * (GLA_HK ** -0.5)
        k = x[:, 128:256]
        v = x[:, 256:512]
        gate = x[:, 512:768]
        z = _dot_hi(sm_ref[b], wup_ref[...]) + bias_ref[...]
        log_a = jax.nn.log_sigmoid(z) / GLA_GATE_NORM
        bc = _cumsum_rows(log_a)
        b_last = bc[L - 1:L, :]
        q_dec = q * jnp.exp(bc)
        k_dec = k * jnp.exp(-bc)
        k_end = k * jnp.exp(b_last - bc)
        s_cat = _dot_nt(q_dec, _stack_heads(k_dec, GLA_HK))
        s_cat = jnp.where(causal, s_cat, 0.0)
        o = _dot(s_cat, _stack_heads(v, GLA_HV))
        st = st_ref[b]
        o = o + _dot_nt(q_dec, st)
        st_ref[b] = st * jnp.exp(b_last) + jnp.where(blockdiag, _dot_tn(v, k_end), 0.0)
        ms = _seg_dot(o * o, ones_ref[...]) * (1.0 / GLA_HV)
        o = o * lax.rsqrt(ms + GROUP_NORM_EPS) * nw_ref[...]
        o_ref[b] = o * jax.nn.silu(gate)


def _gla(p_gla, p_sm, wup_pad, bias, nw, ones64):
    nb, s, _ = p_gla.shape
    L = GLA_CHUNK
    return pl.pallas_call(
        functools.partial(_gla_body, nb=nb),
        grid=(s // L,),
        in_specs=[
            pl.BlockSpec((nb, L, 768), lambda c: (0, c, 0)),
            pl.BlockSpec((nb, L, SM_W), lambda c: (0, c, 0)),
            pl.BlockSpec((SM_W, GLA_DK), lambda c: (0, 0)),
            pl.BlockSpec((1, GLA_DK), lambda c: (0, 0)),
            pl.BlockSpec((1, W_GROUP), lambda c: (0, 0)),
            pl.BlockSpec((W_GROUP, W_GROUP), lambda c: (0, 0)),
        ],
        out_specs=pl.BlockSpec((nb, L, W_GROUP), lambda c: (0, c, 0)),
        out_shape=jax.ShapeDtypeStruct((nb, s, W_GROUP), f32),
        scratch_shapes=[pltpu.VMEM((nb, W_GROUP, GLA_DK), f32)],
        compiler_params=pltpu.CompilerParams(dimension_semantics=("arbitrary",)),
        name="gla",
    )(p_gla, p_sm, wup_pad, bias, nw, ones64)


def _lru_body(p_ref, cw_ref, cb_ref, wax_ref, bax_ref, lam_ref, o_ref, ext_ref, h_ref, *, nb):
    c = pl.program_id(0)
    tc = LRU_TILE
    w = W_GROUP

    @pl.when(c == 0)
    def _():
        for b in range(nb):
            ext_ref[b, 0:8, :] = jnp.zeros((8, w), f32)
        h_ref[...] = jnp.zeros_like(h_ref)

    row = lax.broadcasted_iota(jnp.int32, (tc, w), 0)
    for b in range(nb):
        x = p_ref[b]
        xb = x[:, 0:w]
        gate = x[:, w:2 * w]
        ext_ref[b, 8:, :] = xb
        xc = cb_ref[...] + cw_ref[CONV_W - 1:CONV_W, :] * xb
        for j in range(1, CONV_W):
            xc = xc + cw_ref[CONV_W - 1 - j:CONV_W - j, :] * ext_ref[b, pl.ds(8 - j, tc), :]
        ext_ref[b, 0:8, :] = xb[tc - 8:tc, :]
        ri = jax.nn.sigmoid(_dot(xc, wax_ref[...]) + bax_ref[...])
        r = ri[:, 0:w]
        i = ri[:, w:2 * w]
        log_a = -LRU_C * r * jax.nn.softplus(-lam_ref[...])
        a = jnp.exp(log_a)
        u = jnp.sqrt(1.0 - a * a) * (i * xc)
        s = 1
        while s < tc:
            keep = row >= s
            u = u + a * jnp.where(keep, pltpu.roll(u, s, 0), 0.0)
            a = a * jnp.where(keep, pltpu.roll(a, s, 0), 1.0)
            s *= 2
        h = u + a * h_ref[b]
        h_ref[b] = h[tc - 1:tc, :]
        o_ref[b] = h * jax.nn.gelu(gate)


def _lru(p_lru, cw, cb, wax, bax, sp):
    nb, s, _ = p_lru.shape
    tc = LRU_TILE
    w = W_GROUP
    return pl.pallas_call(
        functools.partial(_lru_body, nb=nb),
        grid=(s // tc,),
        in_specs=[
            pl.BlockSpec((nb, tc, 2 * w), lambda c: (0, c, 0)),
            pl.BlockSpec((CONV_W, w), lambda c: (0, 0)),
            pl.BlockSpec((1, w), lambda c: (0, 0)),
            pl.BlockSpec((w, 2 * w), lambda c: (0, 0)),
            pl.BlockSpec((1, 2 * w), lambda c: (0, 0)),
            pl.BlockSpec((1, w), lambda c: (0, 0)),
        ],
        out_specs=pl.BlockSpec((nb, tc, w), lambda c: (0, c, 0)),
        out_shape=jax.ShapeDtypeStruct((nb, s, w), f32),
        scratch_shapes=[pltpu.VMEM((nb, tc + 8, w), f32), pltpu.VMEM((nb, 1, w), f32)],
        compiler_params=pltpu.CompilerParams(dimension_semantics=("arbitrary",)),
        name="rglru",
    )(p_lru, cw, cb, wax, bax, sp)


def _rw_body(*refs, nb, mix):
    if mix:
        (p_ref, sm_ref, vf_ref, mu_ref, mus_ref, w2_ref, a2_ref, g2_ref, vec_ref, v1_ref, v2_ref,
         ones_ref, lvl_ref, o_ref, cx_ref, cs_ref, ht_ref) = refs
    else:
        (p_ref, sm_ref, mu_ref, mus_ref, w2_ref, a2_ref, g2_ref, vec_ref,
         ones_ref, lvl_ref, o_ref, vo_ref, cx_ref, cs_ref, ht_ref) = refs
    c = pl.program_id(0)
    L = RW_CHUNK
    w = W_GROUP

    @pl.when(c == 0)
    def _():
        cx_ref[...] = jnp.zeros_like(cx_ref)
        cs_ref[...] = jnp.zeros_like(cs_ref)
        ht_ref[...] = jnp.zeros_like(ht_ref)

    row3 = lax.broadcasted_iota(jnp.int32, (L, 3 * w), 0)
    rows = lax.broadcasted_iota(jnp.int32, (L, SM_W), 0)
    rr = lax.broadcasted_iota(jnp.int32, (4 * L, 4 * L), 0)
    cc = lax.broadcasted_iota(jnp.int32, (4 * L, 4 * L), 1)
    strict = (rr % L) > (cc % L)
    incl = (rr % L) >= (cc % L)
    eye = (rr == cc).astype(f32)
    w0, a0, k_k, k_a, r_k, gn_w, gn_b, v0 = (vec_ref[i:i + 1, :] for i in range(8))
    ones_bf = ones_ref[...]

    for b in range(nb):
        x = p_ref[b]
        sm = sm_ref[b]
        x_prev = jnp.where(row3 == 0, cx_ref[b], pltpu.roll(x, 1, 0))
        sm_prev = jnp.where(rows == 0, cs_ref[b], pltpu.roll(sm, 1, 0))
        cx_ref[b] = x[L - 1:L, :]
        cs_ref[b] = sm[L - 1:L, :]
        x = x + (x_prev - x) * mu_ref[...]
        sm = sm + (sm_prev - sm) * mus_ref[...]
        r = x[:, 0:w]
        k = x[:, w:2 * w]
        v = x[:, 2 * w:3 * w]
        log_w = -RW_DECAY_SCALE * jax.nn.sigmoid(w0 + _dot_hi(jnp.tanh(sm), w2_ref[...]))
        a = jax.nn.sigmoid(a0 + _dot_hi(sm, a2_ref[...]))
        g = _dot_hi(jax.nn.sigmoid(sm), g2_ref[...])
        if mix:
            lam = jax.nn.sigmoid(v0 + _dot_hi(_dot_hi(v, v1_ref[...]), v2_ref[...]))
            v = v + (vf_ref[b] - v) * lam
        else:
            vo_ref[b] = v
        kk = k * k_k
        n2 = _seg_dot(kk * kk, ones_bf)
        kk = kk / jnp.maximum(jnp.sqrt(n2), 1e-12)
        k = k * (1.0 + (a - 1.0) * k_a)

        cum = _cumsum_rows(log_w)
        g_in = jnp.exp(cum)
        g_inv = jnp.exp(-cum)
        g_last = g_in[L - 1:L, :]
        al = _stack_heads(-kk * jnp.exp(cum - log_w), RW_HS)
        be = _stack_heads(kk * a * g_inv, RW_HS)
        kt = _stack_heads(k * g_inv, RW_HS)
        rt = _stack_heads(r * g_in, RW_HS)
        vst = _stack_heads(v, RW_HS)
        aa = _dot_nt(jnp.concatenate([al, rt], axis=0), jnp.concatenate([be, kt], axis=0))
        a_ab = jnp.where(strict, aa[0:4 * L, 0:4 * L], 0.0)
        a_ak = jnp.where(strict, aa[0:4 * L, 4 * L:8 * L], 0.0)
        a_rb = jnp.where(incl, aa[4 * L:8 * L, 0:4 * L], 0.0)
        a_rk = jnp.where(incl, aa[4 * L:8 * L, 4 * L:8 * L], 0.0)

        t_inv = eye + a_ab * lvl_ref[0].astype(f32)
        for lv in range(1, 6):
            a_lv = a_ab * lvl_ref[lv].astype(f32)
            t_inv = t_inv + _dot(_dot(t_inv, a_lv), t_inv)

        p_st = _dot(t_inv, al)
        q_st = _dot(t_inv, _dot(a_ak, vst))
        y_k = _dot(a_rk, vst)
        kv_t = _dot_tn(vst, kt * g_last)

        ht = ht_ref[b]
        u = _dot_nt(p_st, ht) + q_st
        y_st = _dot_nt(rt, ht) + _dot(a_rb, u) + y_k
        ht_ref[b] = ht * g_last + _dot_tn(u, be * g_last) + kv_t
        y = y_st[0:L] + y_st[L:2 * L] + y_st[2 * L:3 * L] + y_st[3 * L:4 * L]

        mean = _seg_dot(y, ones_bf) * (1.0 / RW_HS)
        d = y - mean
        var = _seg_dot(d * d, ones_bf) * (1.0 / RW_HS)
        yn = d * lax.rsqrt(var + RW_GN_EPS) * gn_w + gn_b
        bonus = _seg_dot(r * k * r_k, ones_bf) * v
        o_ref[b] = (yn + bonus) * g


def _rwkv(p_rw, p_sm, v_first, mu, mus, w2p, a2p, g2p, vecs, v1p, v2p, ones64, lvl):
    nb, s, _ = p_rw.shape
    L = RW_CHUNK
    w = W_GROUP
    mix = v_first is not None
    blk = lambda width: pl.BlockSpec((nb, L, width), lambda c: (0, c, 0))
    full = lambda a: pl.BlockSpec(a.shape, lambda c: (0,) * a.ndim)
    ins = [p_rw, p_sm] + ([v_first] if mix else []) + [mu, mus, w2p, a2p, g2p, vecs] \
        + ([v1p, v2p] if mix else []) + [ones64, lvl]
    in_specs = [blk(3 * w), blk(SM_W)] + ([blk(w)] if mix else []) + [full(a) for a in ins[(3 if mix else 2):]]
    out_shape = jax.ShapeDtypeStruct((nb, s, w), f32)
    return pl.pallas_call(
        functools.partial(_rw_body, nb=nb, mix=mix),
        grid=(s // L,),
        in_specs=in_specs,
        out_specs=blk(w) if mix else [blk(w), blk(w)],
        out_shape=out_shape if mix else [out_shape, out_shape],
        scratch_shapes=[pltpu.VMEM((nb, 1, 3 * w), f32), pltpu.VMEM((nb, 1, SM_W), f32),
                        pltpu.VMEM((nb, w, w), f32)],
        compiler_params=pltpu.CompilerParams(
            dimension_semantics=("arbitrary",), vmem_limit_bytes=VMEM_LIMIT),
        name="rwkv7",
    )(*ins)


def _ssd_body(p_ref, sm_ref, cw_ref, cb_ref, dtb_ref, alog_ref, dsk_ref, nw_ref, o_ref, ext_ref, h_ref, *, nb):
    c = pl.program_id(0)
    L = SSD_CHUNK
    w = W_GROUP
    cdim = 3 * w

    @pl.when(c == 0)
    def _():
        for b in range(nb):
            ext_ref[b, 0:8, :] = jnp.zeros((8, cdim), f32)
        h_ref[...] = jnp.zeros_like(h_ref)

    ll = lax.broadcasted_iota(jnp.int32, (L, L), 0)
    ss = lax.broadcasted_iota(jnp.int32, (L, L), 1)
    causal = ll >= ss
    lane_head = lax.broadcasted_iota(jnp.int32, (L, w), 1) // (w // SSD_HEADS)

    for b in range(nb):
        x = p_ref[b]
        z = x[:, 0:w]
        xbc = x[:, w:4 * w]
        ext_ref[b, 8:, :] = xbc
        conv = cb_ref[...] + cw_ref[CONV_W - 1:CONV_W, :] * xbc
        for j in range(1, CONV_W):
            conv = conv + cw_ref[CONV_W - 1 - j:CONV_W - j, :] * ext_ref[b, pl.ds(8 - j, L), :]
        ext_ref[b, 0:8, :] = xbc[L - 8:L, :]
        xbc = jax.nn.silu(conv)
        xs = xbc[:, 0:w]
        bm = xbc[:, w:2 * w]
        cm = xbc[:, 2 * w:3 * w]

        dt = jax.nn.softplus(sm_ref[b] + dtb_ref[...])
        d_a = dt * (-jnp.exp(alog_ref[0:1, :]) * alog_ref[1:2, :])
        cs = _cumsum_rows(d_a)
        cs_t = cs.T
        dt_full = _expand_heads(dt, SM_DT, w)
        cs_full = _expand_heads(cs, SM_DT, w)
        cs_last = cs_full[L - 1:L, :]
        xdt = xs * dt_full

        h_prev = h_ref[b]
        bm_t = bm.T
        y = jnp.zeros((L, w), f32)
        y_off = []
        st = []
        xw = xdt * jnp.exp(cs_last - cs_full)
        for grp in range(2):
            lo, hi = grp * SSD_DSTATE, (grp + 1) * SSD_DSTATE
            cb_g = _dot_nt(cm[:, lo:hi], bm[:, lo:hi])
            for r in range(2):
                hd = grp * 2 + r
                seg = cs[:, SM_DT + hd:SM_DT + hd + 1] - cs_t[SM_DT + hd:SM_DT + hd + 1, :]
                m = cb_g * jnp.where(causal, jnp.exp(seg), 0.0)
                y = y + _dot(m, jnp.where(lane_head == hd, xdt, 0.0))
            y_off.append(_dot(cm[:, lo:hi], h_prev[:, lo:hi]))
            st.append(_dot(bm_t[lo:hi, :], xw[:, lo:hi]))
        y = y + jnp.concatenate(y_off, axis=1) * jnp.exp(cs_full)
        h_ref[b] = h_prev * jnp.exp(cs_last) + jnp.concatenate(st, axis=1)
        y = y + dsk_ref[...] * xs
        y = y * jax.nn.silu(z)
        outs = []
        for grp in range(2):
            yg = y[:, grp * 128:(grp + 1) * 128]
            ms = jnp.mean(yg * yg, axis=-1, keepdims=True)
            outs.append(yg * lax.rsqrt(ms + GROUP_NORM_EPS))
        o_ref[b] = jnp.concatenate(outs, axis=1) * nw_ref[...]


def _ssd(p_ssd, p_sm, cw, cb, dtb_row, a_row, dsk, nw):
    nb, s, _ = p_ssd.shape
    L = SSD_CHUNK
    w = W_GROUP
    full = lambda a: pl.BlockSpec(a.shape, lambda c: (0,) * a.ndim)
    return pl.pallas_call(
        functools.partial(_ssd_body, nb=nb),
        grid=(s // L,),
        in_specs=[pl.BlockSpec((nb, L, 4 * w), lambda c: (0, c, 0)),
                  pl.BlockSpec((nb, L, SM_W), lambda c: (0, c, 0))]
        + [full(a) for a in (cw, cb, dtb_row, a_row, dsk, nw)],
        out_specs=pl.BlockSpec((nb, L, w), lambda c: (0, c, 0)),
        out_shape=jax.ShapeDtypeStruct((nb, s, w), f32),
        scratch_shapes=[pltpu.VMEM((nb, L + 8, 3 * w), f32), pltpu.VMEM((nb, SSD_DSTATE, w), f32)],
        compiler_params=pltpu.CompilerParams(dimension_semantics=("arbitrary",)),
        name="ssd",
    )(p_ssd, p_sm, cw, cb, dtb_row, a_row, dsk, nw)


def _level_masks():
    n = 4 * RW_CHUNK
    t = np.arange(n)[:, None]
    u = np.arange(n)[None, :]
    out = []
    s = 1
    while s < RW_CHUNK:
        out.append(((t // (2 * s)) == (u // (2 * s))) & ((t % (2 * s)) >= s) & ((u % (2 * s)) < s))
        s *= 2
    return jnp.asarray(np.stack(out).astype(np.float32), dtype=bf16)


def _ones_blocks(seg):
    i = np.arange(W_GROUP)
    return jnp.asarray((i[:, None] // seg == i[None, :] // seg).astype(np.float32), dtype=bf16)


def _pad_rows(m, row0, rows):
    return jnp.zeros((rows, m.shape[1]), m.dtype).at[row0:row0 + m.shape[0], :].set(m)


def _lane_row(v, lane0, width):
    return jnp.zeros((1, width), f32).at[0, lane0:lane0 + v.shape[0]].set(v)


def _block_diag(w):
    n, k, j = w.shape
    out = jnp.zeros((n * k, n * j), w.dtype)
    for i in range(n):
        out = out.at[i * k:(i + 1) * k, i * j:(i + 1) * j].set(w[i])
    return out


def kernel(x, ffn1_norm, ffn1_w_gate, ffn1_w_up, ffn1_w_down, mix_norm, w_in, w_out, gla_alpha_up, gla_alpha_bias, gla_norm, lru_conv_w, lru_conv_b, lru_w_a, lru_b_a, lru_w_x, lru_b_x, lru_lambda, rw_mu, rw_w0, rw_w2, rw_a0, rw_a2, rw_g2, rw_v0, rw_v1, rw_v2, rw_k_k, rw_k_a, rw_r_k, rw_gn_w, rw_gn_b, ssd_conv_w, ssd_conv_b, ssd_dt_bias, ssd_a_log, ssd_d, ssd_norm, ffn2_norm, ffn2_w_gate, ffn2_w_up, ffn2_w_down, final_norm):
    nb, s, d = x.shape
    t = nb * s
    depth = w_in.shape[0]
    tm = min(512, t)
    tm_in = min(256, t)
    tf = D_FF // 2

    wg1, wu1, wd1 = (a.astype(bf16) for a in (ffn1_w_gate, ffn1_w_up, ffn1_w_down))
    wg2, wu2, wd2 = (a.astype(bf16) for a in (ffn2_w_gate, ffn2_w_up, ffn2_w_down))
    w_out_b = w_out.astype(bf16)
    pad = jnp.zeros((depth, d, SM_W - 84), w_in.dtype)
    w_in_p = jnp.concatenate([
        w_in[:, :, _GLA0:_GLA0 + 768], w_in[:, :, _LRU0:_LRU0 + 512], w_in[:, :, _RW0:_RW0 + 768],
        w_in[:, :, _SSD0:_SSD0 + 1024],
        w_in[:, :, _GLA0 + 768:_GLA0 + 784], w_in[:, :, _RW0 + 768:_RW0 + 832],
        w_in[:, :, _SSD0 + 1024:_SSD0 + 1028], pad], axis=-1).astype(bf16)
    ones64 = _ones_blocks(64)
    lvl = _level_masks()

    x2 = x.reshape(t, d)
    v_first = None
    for l in range(depth):
        x2 = _ffn(x2, ffn1_norm[l][None, :], wg1, wu1, wd1, l, tm=tm, tf=tf)
        p_gla, p_lru, p_rw, p_ssd, p_sm = _inproj(x2, mix_norm[l][None, :], w_in_p, l, tm=tm_in)
        p_gla, p_lru, p_rw, p_ssd, p_sm = (a.reshape(nb, s, a.shape[-1]) for a in (p_gla, p_lru, p_rw, p_ssd, p_sm))

        y_gla = _gla(p_gla, p_sm, _pad_rows(gla_alpha_up[l], SM_GLA, SM_W), gla_alpha_bias[l][None, :],
                     jnp.tile(gla_norm[l], GLA_HEADS)[None, :], ones64)

        wax = jnp.concatenate([_block_diag(lru_w_a[l]), _block_diag(lru_w_x[l])], axis=1).astype(bf16)
        bax = jnp.concatenate([lru_b_a[l], lru_b_x[l]])[None, :]
        y_lru = _lru(p_lru, lru_conv_w[l], lru_conv_b[l][None, :], wax, bax,
                     lru_lambda[l][None, :])

        vecs = jnp.stack([rw_w0[l], rw_a0[l], rw_k_k[l], rw_k_a[l], rw_r_k[l].reshape(-1), rw_gn_w[l],
                          rw_gn_b[l], rw_v0[l - 1] if l > 0 else jnp.zeros_like(rw_w0[l])])
        mu = rw_mu[l]
        rw_args = (mu[None, 0:768], _lane_row(mu[768:832], SM_RW, SM_W),
                   _pad_rows(rw_w2[l], SM_RW, SM_W), _pad_rows(rw_a2[l], SM_RW + 16, SM_W),
                   _pad_rows(rw_g2[l], SM_RW + 32, SM_W), vecs)
        if l == 0:
            y_rw, v_first = _rwkv(p_rw, p_sm, None, *rw_args, None, None, ones64, lvl)
        else:
            v1p = jnp.zeros((W_GROUP, 128), f32).at[:, 0:rw_v1.shape[-1]].set(rw_v1[l - 1])
            v2p = _pad_rows(rw_v2[l - 1], 0, 128)
            y_rw = _rwkv(p_rw, p_sm, v_first, *rw_args, v1p, v2p, ones64, lvl)

        y_ssd = _ssd(p_ssd, p_sm, ssd_conv_w[l], ssd_conv_b[l][None, :],
                     _lane_row(ssd_dt_bias[l], SM_DT, SM_W), jnp.concatenate([_lane_row(ssd_a_log[l], SM_DT, SM_W),
                                      _lane_row(jnp.ones((SSD_HEADS,), f32), SM_DT, SM_W)]),
                     jnp.repeat(ssd_d[l], W_GROUP // SSD_HEADS)[None, :], ssd_norm[l][None, :])

        ys = [a.reshape(t, W_GROUP) for a in (y_gla, y_lru, y_rw, y_ssd)]
        x2 = _outproj(x2, ys, w_out_b, l, tm=tm)
        x2 = _ffn(x2, ffn2_norm[l][None, :], wg2, wu2, wd2, l, tm=tm, tf=tf)
    return _final_norm(x2, final_norm[None, :], tm=tm).reshape(nb, s, d)
```

```python
import functools
import math

import numpy as np
import jax
import jax.numpy as jnp
from jax import lax
from jax.experimental import pallas as pl
from jax.experimental.pallas import tpu as pltpu

f32 = jnp.float32
bf16 = jnp.bfloat16

D_MODEL = 1024
DEPTH = 4
D_FF = 2816
W_GROUP = 256
NORM_EPS = 1e-6
GROUP_NORM_EPS = 1e-5
CONV_W = 4

GLA_HEADS = 4
GLA_DK = 128
GLA_HK = 32
GLA_HV = 64
GLA_RANK = 16
GLA_GATE_NORM = 16.0
GLA_CHUNK = 64
GLA_TILE = 128

LRU_C = 8.0

RW_HS = 64
RW_DECAY_SCALE = math.exp(-0.5)
RW_GN_EPS = 64e-5
RW_CHUNK = 64
RW_TILE = 128

SSD_HEADS = 4
SSD_DSTATE = 128
SSD_CHUNK = 128

_GLA0, _LRU0, _RW0, _SSD0, _N_IN = 0, 784, 1296, 2128, 3156
SM_GLA = 0
SM_RW = 16
SM_DT = 80
SM_W = 128

LRU_TILE = 256
VMEM_LIMIT = 56 * 1024 * 1024

def _dot(a, b):
    return jnp.dot(a.astype(bf16), b.astype(bf16), preferred_element_type=f32)


def _dot_nt(a, b):
    return lax.dot_general(a.astype(bf16), b.astype(bf16), (((1,), (1,)), ((), ())),
                           preferred_element_type=f32)


def _dot_tn(a, b):
    return lax.dot_general(a.astype(bf16), b.astype(bf16), (((0,), (0,)), ((), ())),
                           preferred_element_type=f32)


def _seg_dot(x, ones_bf):
    hi = x.astype(bf16)
    lo = (x - hi.astype(f32)).astype(bf16)
    return (jnp.dot(hi, ones_bf, preferred_element_type=f32)
            + jnp.dot(lo, ones_bf, preferred_element_type=f32))


def _cumsum_rows(x, seg=None):
    seg = x.shape[0] if seg is None else seg
    row = lax.broadcasted_iota(jnp.int32, x.shape, 0) % seg
    s = 1
    while s < seg:
        x = x + jnp.where(row >= s, pltpu.roll(x, s, 0), 0.0)
        s *= 2
    return x


def _head_masks(rows, width):
    lane = lax.broadcasted_iota(jnp.int32, (rows, width), 1) // (width // 4)
    return [(lane == h).astype(f32).astype(bf16) for h in range(4)]


def _stack_heads(x, masks):
    xb = x.astype(bf16)
    return jnp.concatenate([xb * m for m in masks], axis=0)


def _expand_heads(x, lane0, width):
    n = x.shape[0]
    lane = lax.broadcasted_iota(jnp.int32, (n, width), 1) // (width // 4)
    out = jnp.zeros((n, width), f32)
    for h in range(4):
        out = jnp.where(lane == h, x[:, lane0 + h:lane0 + h + 1], out)
    return out


def _ffn_body(x_ref, nw_ref, wg_ref, wu_ref, wd_ref, o_ref, xn_ref, acc_ref, *, nj):
    j = pl.program_id(1)

    @pl.when(j == 0)
    def _():
        x = x_ref[...]
        ms = jnp.mean(x * x, axis=-1, keepdims=True)
        xn_ref[...] = (x * lax.rsqrt(ms + NORM_EPS) * nw_ref[...]).astype(bf16)
        acc_ref[...] = jnp.zeros_like(acc_ref)

    xn = xn_ref[...]
    g = jnp.dot(xn, wg_ref[...], preferred_element_type=f32)
    u = jnp.dot(xn, wu_ref[...], preferred_element_type=f32)
    h = (jax.nn.silu(g) * u).astype(bf16)
    acc_ref[...] += jnp.dot(h, wd_ref[...], preferred_element_type=f32)

    @pl.when(j == nj - 1)
    def _():
        o_ref[...] = x_ref[...] + 0.5 * acc_ref[...]


def _ffn(x2, nw, wg, wu, wd, layer, *, tm, tf):
    t, d = x2.shape
    nj = wg.shape[-1] // tf
    return pl.pallas_call(
        functools.partial(_ffn_body, nj=nj),
        grid=(t // tm, nj),
        in_specs=[
            pl.BlockSpec((tm, d), lambda i, j: (i, 0)),
            pl.BlockSpec((1, d), lambda i, j: (0, 0)),
            pl.BlockSpec((None, d, tf), lambda i, j: (layer, 0, j)),
            pl.BlockSpec((None, d, tf), lambda i, j: (layer, 0, j)),
            pl.BlockSpec((None, tf, d), lambda i, j: (layer, j, 0)),
        ],
        out_specs=pl.BlockSpec((tm, d), lambda i, j: (i, 0)),
        out_shape=jax.ShapeDtypeStruct((t, d), f32),
        scratch_shapes=[pltpu.VMEM((tm, d), bf16), pltpu.VMEM((tm, d), f32)],
        compiler_params=pltpu.CompilerParams(
            dimension_semantics=("parallel", "arbitrary"), vmem_limit_bytes=VMEM_LIMIT),
        name="ffn",
    )(x2, nw, wg, wu, wd)


_SEG_W = (768, 512, 768, 1024, SM_W)
_SEG_O = tuple(int(o) for o in np.cumsum((0,) + _SEG_W))


def _inproj_body(x_ref, nw_ref, w_ref, o_gla, o_lru, o_rw, o_ssd, o_sm):
    x = x_ref[...]
    ms = jnp.mean(x * x, axis=-1, keepdims=True)
    xn = (x * lax.rsqrt(ms + NORM_EPS) * nw_ref[...]).astype(bf16)
    for o_ref, lo, hi in zip((o_gla, o_lru, o_rw, o_ssd, o_sm), _SEG_O[:-1], _SEG_O[1:]):
        o_ref[...] = jnp.dot(xn, w_ref[:, lo:hi], preferred_element_type=f32)


def _inproj(x2, nw, w_in_p, layer, *, tm):
    t, d = x2.shape
    n = w_in_p.shape[-1]
    return pl.pallas_call(
        _inproj_body,
        grid=(t // tm,),
        in_specs=[
            pl.BlockSpec((tm, d), lambda i: (i, 0)),
            pl.BlockSpec((1, d), lambda i: (0, 0)),
            pl.BlockSpec((None, d, n), lambda i: (layer, 0, 0)),
        ],
        out_specs=[pl.BlockSpec((tm, w), lambda i: (i, 0)) for w in _SEG_W],
        out_shape=[jax.ShapeDtypeStruct((t, w), f32) for w in _SEG_W],
        compiler_params=pltpu.CompilerParams(
            dimension_semantics=("parallel",), vmem_limit_bytes=VMEM_LIMIT),
        name="inproj",
    )(x2, nw, w_in_p)


def _outproj_body(x_ref, y0, y1, y2, y3, w_ref, o_ref):
    acc = x_ref[...]
    for i, y in enumerate((y0, y1, y2, y3)):
        acc = acc + jnp.dot(y[...].astype(bf16), w_ref[i * W_GROUP:(i + 1) * W_GROUP, :],
                            preferred_element_type=f32)
    o_ref[...] = acc


def _outproj(x2, ys, w_out, layer, *, tm):
    t, d = x2.shape
    return pl.pallas_call(
        _outproj_body,
        grid=(t // tm,),
        in_specs=[pl.BlockSpec((tm, d), lambda i: (i, 0))]
        + [pl.BlockSpec((tm, W_GROUP), lambda i: (i, 0)) for _ in range(4)]
        + [pl.BlockSpec((None, d, d), lambda i: (layer, 0, 0))],
        out_specs=pl.BlockSpec((tm, d), lambda i: (i, 0)),
        out_shape=jax.ShapeDtypeStruct((t, d), f32),
        compiler_params=pltpu.CompilerParams(
            dimension_semantics=("parallel",), vmem_limit_bytes=VMEM_LIMIT),
        name="outproj",
    )(x2, *ys, w_out)


def _final_norm_body(x_ref, nw_ref, o_ref):
    x = x_ref[...]
    ms = jnp.mean(x * x, axis=-1, keepdims=True)
    o_ref[...] = x * lax.rsqrt(ms + NORM_EPS) * nw_ref[...]


def _final_norm(x2, nw, *, tm):
    t, d = x2.shape
    return pl.pallas_call(
        _final_norm_body,
        grid=(t // tm,),
        in_specs=[pl.BlockSpec((tm, d), lambda i: (i, 0)), pl.BlockSpec((1, d), lambda i: (0, 0))],
        out_specs=pl.BlockSpec((tm, d), lambda i: (i, 0)),
        out_shape=jax.ShapeDtypeStruct((t, d), f32),
        compiler_params=pltpu.CompilerParams(dimension_semantics=("parallel",)),
        name="final_norm",
    )(x2, nw)


def _gla_body(p_ref, sm_ref, wup_ref, bias_ref, nw_ref, ones_ref, o_ref, st_ref, *, nb):
    c = pl.program_id(0)
    L = GLA_CHUNK
    tc = GLA_TILE
    nch = tc // L

    @pl.when(c == 0)
    def _():
        st_ref[...] = jnp.zeros_like(st_ref)

    row = lax.broadcasted_iota(jnp.int32, (L, 4 * L), 0)
    col = lax.broadcasted_iota(jnp.int32, (L, 4 * L), 1)
    causal = (col % L) <= row
    srow = lax.broadcasted_iota(jnp.int32, (W_GROUP, GLA_DK), 0) // GLA_HV
    scol = lax.broadcasted_iota(jnp.int32, (W_GROUP, GLA_DK), 1) // GLA_HK
    blockdiag = srow == scol
    k_masks = _head_masks(L, GLA_DK)
    v_masks = _head_masks(L, W_GROUP)

    pre = []
    for b in range(nb):
        x = p_ref[b]
        z = _dot(sm_ref[b], wup_ref[...]) + bias_ref[...]
        bc = _cumsum_rows(jax.nn.log_sigmoid(z) / GLA_GATE_NORM, L)
        q_dec = (x[:, 0:128] * (GLA_HK ** -0.5) * jnp.exp(bc)).astype(bf16)
        pre.append((x, bc, q_dec, x[:, 128:256] * jnp.exp(-bc)))
    chains = [(j, b) for j in range(nch) for b in range(nb)]
    intra, kv, dec = {}, {}, {}
    for j, b in chains:
        x, bc, q_dec, k_dec = pre[b]
        rows = slice(j * L, (j + 1) * L)
        s_cat = _dot_nt(q_dec[rows], _stack_heads(k_dec[rows], k_masks))
        s_cat = jnp.where(causal, s_cat, 0.0)
        v = x[rows, 256:512]
        intra[j, b] = _dot(s_cat, _stack_heads(v, v_masks))
        b_last = bc[(j + 1) * L - 1:(j + 1) * L, :]
        k_end = x[rows, 128:256] * jnp.exp(b_last - bc[rows])
        kv[j, b] = jnp.where(blockdiag, _dot_tn(v, k_end), 0.0)
        dec[j, b] = jnp.exp(b_last)
    st = [st_ref[b] for b in range(nb)]
    outs = [[None] * nch for _ in range(nb)]
    for j, b in chains:
        q_dec = pre[b][2]
        outs[b][j] = intra[j, b] + _dot_nt(q_dec[j * L:(j + 1) * L], st[b])
        st[b] = st[b] * dec[j, b] + kv[j, b]
    for b in range(nb):
        st_ref[b] = st[b]
        o = jnp.concatenate(outs[b], axis=0)
        ms = _seg_dot(o * o, ones_ref[...]) * (1.0 / GLA_HV)
        o = o * lax.rsqrt(ms + GROUP_NORM_EPS) * nw_ref[...]
        o_ref[b] = o * jax.nn.silu(pre[b][0][:, 512:768])


def _gla(p_gla, p_sm, wup_pad, bias, nw, ones64):
    nb, s, _ = p_gla.shape
    tc = GLA_TILE
    return pl.pallas_call(
        functools.partial(_gla_body, nb=nb),
        grid=(s // tc,),
        in_specs=[
            pl.BlockSpec((nb, tc, 768), lambda c: (0, c, 0)),
            pl.BlockSpec((nb, tc, SM_W), lambda c: (0, c, 0)),
            pl.BlockSpec((SM_W, GLA_DK), lambda c: (0, 0)),
            pl.BlockSpec((1, GLA_DK), lambda c: (0, 0)),
            pl.BlockSpec((1, W_GROUP), lambda c: (0, 0)),
            pl.BlockSpec((W_GROUP, W_GROUP), lambda c: (0, 0)),
        ],
        out_specs=pl.BlockSpec((nb, tc, W_GROUP), lambda c: (0, c, 0)),
        out_shape=jax.ShapeDtypeStruct((nb, s, W_GROUP), f32),
        scratch_shapes=[pltpu.VMEM((nb, W_GROUP, GLA_DK), f32)],
        compiler_params=pltpu.CompilerParams(dimension_semantics=("arbitrary",)),
        name="gla",
    )(p_gla, p_sm, wup_pad, bias, nw, ones64)


def _lru_body(p_ref, cw_ref, cb_ref, wax_ref, bax_ref, lam_ref, o_ref, ext_ref, h_ref, *, nb):
    c = pl.program_id(0)
    tc = LRU_TILE
    w = W_GROUP

    @pl.when(c == 0)
    def _():
        for b in range(nb):
            ext_ref[b, 0:8, :] = jnp.zeros((8, w), f32)
        h_ref[...] = jnp.zeros_like(h_ref)

    row = lax.broadcasted_iota(jnp.int32, (tc, w), 0)
    for b in range(nb):
        x = p_ref[b]
        xb = x[:, 0:w]
        gate = x[:, w:2 * w]
        ext_ref[b, 8:, :] = xb
        xc = cb_ref[...] + cw_ref[CONV_W - 1:CONV_W, :] * xb
        for j in range(1, CONV_W):
            xc = xc + cw_ref[CONV_W - 1 - j:CONV_W - j, :] * ext_ref[b, pl.ds(8 - j, tc), :]
        ext_ref[b, 0:8, :] = xb[tc - 8:tc, :]
        ri = jax.nn.sigmoid(_dot(xc, wax_ref[...]) + bax_ref[...])
        r = ri[:, 0:w]
        i = ri[:, w:2 * w]
        log_a = -LRU_C * r * jax.nn.softplus(-lam_ref[...])
        a = jnp.exp(log_a)
        u = jnp.sqrt(1.0 - a * a) * (i * xc)
        s = 1
        while s < tc:
            keep = row >= s
            u = u + a * jnp.where(keep, pltpu.roll(u, s, 0), 0.0)
            a = a * jnp.where(keep, pltpu.roll(a, s, 0), 1.0)
            s *= 2
        h = u + a * h_ref[b]
        h_ref[b] = h[tc - 1:tc, :]
        o_ref[b] = h * jax.nn.gelu(gate)


def _lru(p_lru, cw, cb, wax, bax, sp):
    nb, s, _ = p_lru.shape
    tc = LRU_TILE
    w = W_GROUP
    return pl.pallas_call(
        functools.partial(_lru_body, nb=nb),
        grid=(s // tc,),
        in_specs=[
            pl.BlockSpec((nb, tc, 2 * w), lambda c: (0, c, 0)),
            pl.BlockSpec((CONV_W, w), lambda c: (0, 0)),
            pl.BlockSpec((1, w), lambda c: (0, 0)),
            pl.BlockSpec((w, 2 * w), lambda c: (0, 0)),
            pl.BlockSpec((1, 2 * w), lambda c: (0, 0)),
            pl.BlockSpec((1, w), lambda c: (0, 0)),
        ],
        out_specs=pl.BlockSpec((nb, tc, w), lambda c: (0, c, 0)),
        out_shape=jax.ShapeDtypeStruct((nb, s, w), f32),
        scratch_shapes=[pltpu.VMEM((nb, tc + 8, w), f32), pltpu.VMEM((nb, 1, w), f32)],
        compiler_params=pltpu.CompilerParams(dimension_semantics=("arbitrary",)),
        name="rglru",
    )(p_lru, cw, cb, wax, bax, sp)


def _rw_body(*refs, nb, mix):
    if mix:
        (p_ref, sm_ref, vf_ref, mu_ref, mus_ref, w2_ref, a2_ref, g2_ref, vec_ref, v1_ref, v2_ref,
         ones_ref, lvl_ref, o_ref, cx_ref, cs_ref, ht_ref) = refs
    else:
        (p_ref, sm_ref, mu_ref, mus_ref, w2_ref, a2_ref, g2_ref, vec_ref,
         ones_ref, lvl_ref, o_ref, vo_ref, cx_ref, cs_ref, ht_ref) = refs
    c = pl.program_id(0)
    L = RW_CHUNK
    tc = RW_TILE
    nch = tc // L
    w = W_GROUP
    n4 = 4 * L

    @pl.when(c == 0)
    def _():
        cx_ref[...] = jnp.zeros_like(cx_ref)
        cs_ref[...] = jnp.zeros_like(cs_ref)
        ht_ref[...] = jnp.zeros_like(ht_ref)

    row3 = lax.broadcasted_iota(jnp.int32, (tc, 3 * w), 0)
    rows_sm = lax.broadcasted_iota(jnp.int32, (tc, SM_W), 0)
    rr = lax.broadcasted_iota(jnp.int32, (n4, n4), 0)
    cc = lax.broadcasted_iota(jnp.int32, (n4, n4), 1)
    strict = (rr % L) > (cc % L)
    incl = (rr % L) >= (cc % L)
    eye = (rr == cc).astype(f32).astype(bf16)
    masks = _head_masks(L, w)
    w0, a0, k_k, k_a, r_k, gn_w, gn_b, v0 = (vec_ref[i:i + 1, :] for i in range(8))
    ones_bf = ones_ref[...]

    pre = []
    for b in range(nb):
        x = p_ref[b]
        sm = sm_ref[b]
        x_prev = jnp.where(row3 == 0, cx_ref[b], pltpu.roll(x, 1, 0))
        sm_prev = jnp.where(rows_sm == 0, cs_ref[b], pltpu.roll(sm, 1, 0))
        cx_ref[b] = x[tc - 1:tc, :]
        cs_ref[b] = sm[tc - 1:tc, :]
        x = x + (x_prev - x) * mu_ref[...]
        sm = sm + (sm_prev - sm) * mus_ref[...]
        r = x[:, 0:w]
        k = x[:, w:2 * w]
        v = x[:, 2 * w:3 * w]
        log_w = -RW_DECAY_SCALE * jax.nn.sigmoid(w0 + _dot(jnp.tanh(sm), w2_ref[...]))
        a = jax.nn.sigmoid(a0 + _dot(sm, a2_ref[...]))
        g = _dot(jax.nn.sigmoid(sm), g2_ref[...])
        if mix:
            lam = jax.nn.sigmoid(v0 + _dot(_dot(v, v1_ref[...]), v2_ref[...]))
            v = v + (vf_ref[b] - v) * lam
        else:
            vo_ref[b] = v
        kk = k * k_k
        n2 = _seg_dot(kk * kk, ones_bf)
        kk = kk / jnp.maximum(jnp.sqrt(n2), 1e-12)
        k = k * (1.0 + (a - 1.0) * k_a)
        cum = _cumsum_rows(log_w, L)
        g_in = jnp.exp(cum)
        g_inv = jnp.exp(-cum)
        pre.append(dict(r=r, k=k, v=v, g=g, g_in=g_in,
                        al=-kk * jnp.exp(cum - log_w),
                        be=kk * a * g_inv, kt=k * g_inv, rt=r * g_in))

    chains = [(j, b) for j in range(nch) for b in range(nb)]
    ch = {}
    for j, b in chains:
        p = pre[b]
        rows = slice(j * L, (j + 1) * L)
        g_last = p["g_in"][(j + 1) * L - 1:(j + 1) * L, :]
        d = dict(g_last=g_last)
        d["al"] = _stack_heads(p["al"][rows], masks)
        d["rt"] = _stack_heads(p["rt"][rows], masks)
        d["vst"] = _stack_heads(p["v"][rows], masks)
        be = _stack_heads(p["be"][rows], masks)
        kt = _stack_heads(p["kt"][rows], masks)
        d["be_end"] = _stack_heads(p["be"][rows] * g_last, masks)
        d["kt_end"] = _stack_heads(p["kt"][rows] * g_last, masks)
        aa = _dot_nt(jnp.concatenate([d["al"], d["rt"]], axis=0), jnp.concatenate([be, kt], axis=0))
        d["a_ab"] = jnp.where(strict, aa[0:n4, 0:n4], 0.0).astype(bf16)
        d["a_ak"] = jnp.where(strict, aa[0:n4, n4:2 * n4], 0.0).astype(bf16)
        d["a_rb"] = jnp.where(incl, aa[n4:2 * n4, 0:n4], 0.0).astype(bf16)
        d["a_rk"] = jnp.where(incl, aa[n4:2 * n4, n4:2 * n4], 0.0).astype(bf16)
        ch[j, b] = d

    for key in chains:
        ch[key]["t"] = eye + ch[key]["a_ab"] * lvl_ref[0]
    for lv in range(1, 6):
        x1 = {key: _dot(ch[key]["t"], ch[key]["a_ab"] * lvl_ref[lv]) for key in chains}
        for key in chains:
            ch[key]["t"] = ch[key]["t"] + _dot(x1[key], ch[key]["t"]).astype(bf16)

    for key in chains:
        d = ch[key]
        akv = _dot(d["a_ak"], d["vst"]).astype(bf16)
        pq = _dot(d["t"], jnp.concatenate([d["al"], akv], axis=1))
        d["p_st"] = pq[:, 0:w].astype(bf16)
        d["q_st"] = pq[:, w:2 * w]
        d["y_k"] = _dot(d["a_rk"], d["vst"])
        d["kv_t"] = _dot_tn(d["vst"], d["kt_end"])

    ht = [ht_ref[b] for b in range(nb)]
    ys = [[None] * nch for _ in range(nb)]
    for j, b in chains:
        d = ch[j, b]
        ph = _dot_nt(jnp.concatenate([d["p_st"], d["rt"]], axis=0), ht[b])
        u = (ph[0:n4] + d["q_st"]).astype(bf16)
        y_st = ph[n4:2 * n4] + _dot(d["a_rb"], u) + d["y_k"]
        ht[b] = ht[b] * d["g_last"] + _dot_tn(u, d["be_end"]) + d["kv_t"]
        ys[b][j] = y_st[0:L] + y_st[L:2 * L] + y_st[2 * L:3 * L] + y_st[3 * L:4 * L]

    for b in range(nb):
        ht_ref[b] = ht[b]
        p = pre[b]
        y = jnp.concatenate(ys[b], axis=0)
        mean = _seg_dot(y, ones_bf) * (1.0 / RW_HS)
        dlt = y - mean
        var = _seg_dot(dlt * dlt, ones_bf) * (1.0 / RW_HS)
        yn = dlt * lax.rsqrt(var + RW_GN_EPS) * gn_w + gn_b
        bonus = _seg_dot(p["r"] * p["k"] * r_k, ones_bf) * p["v"]
        o_ref[b] = (yn + bonus) * p["g"]


def _rwkv(p_rw, p_sm, v_first, mu, mus, w2p, a2p, g2p, vecs, v1p, v2p, ones64, lvl):
    nb, s, _ = p_rw.shape
    tc = RW_TILE
    w = W_GROUP
    mix = v_first is not None
    blk = lambda width: pl.BlockSpec((nb, tc, width), lambda c: (0, c, 0))
    full = lambda a: pl.BlockSpec(a.shape, lambda c: (0,) * a.ndim)
    ins = [p_rw, p_sm] + ([v_first] if mix else []) + [mu, mus, w2p, a2p, g2p, vecs] \
        + ([v1p, v2p] if mix else []) + [ones64, lvl]
    in_specs = [blk(3 * w), blk(SM_W)] + ([blk(w)] if mix else []) + [full(a) for a in ins[(3 if mix else 2):]]
    out_shape = jax.ShapeDtypeStruct((nb, s, w), f32)
    return pl.pallas_call(
        functools.partial(_rw_body, nb=nb, mix=mix),
        grid=(s // tc,),
        in_specs=in_specs,
        out_specs=blk(w) if mix else [blk(w), blk(w)],
        out_shape=out_shape if mix else [out_shape, out_shape],
        scratch_shapes=[pltpu.VMEM((nb, 1, 3 * w), f32), pltpu.VMEM((nb, 1, SM_W), f32),
                        pltpu.VMEM((nb, w, w), f32)],
        compiler_params=pltpu.CompilerParams(
            dimension_semantics=("arbitrary",), vmem_limit_bytes=VMEM_LIMIT),
        name="rwkv7",
    )(*ins)


def _ssd_body(p_ref, sm_ref, cw_ref, cb_ref, dtb_ref, alog_ref, dsk_ref, nw_ref, o_ref, ext_ref, h_ref, *, nb):
    c = pl.program_id(0)
    L = SSD_CHUNK
    w = W_GROUP
    cdim = 3 * w

    @pl.when(c == 0)
    def _():
        for b in range(nb):
            ext_ref[b, 0:8, :] = jnp.zeros((8, cdim), f32)
        h_ref[...] = jnp.zeros_like(h_ref)

    ll = lax.broadcasted_iota(jnp.int32, (L, L), 0)
    ss = lax.broadcasted_iota(jnp.int32, (L, L), 1)
    causal = ll >= ss
    lane_head = lax.broadcasted_iota(jnp.int32, (L, w), 1) // (w // SSD_HEADS)

    for b in range(nb):
        x = p_ref[b]
        z = x[:, 0:w]
        xbc = x[:, w:4 * w]
        ext_ref[b, 8:, :] = xbc
        conv = cb_ref[...] + cw_ref[CONV_W - 1:CONV_W, :] * xbc
        for j in range(1, CONV_W):
            conv = conv + cw_ref[CONV_W - 1 - j:CONV_W - j, :] * ext_ref[b, pl.ds(8 - j, L), :]
        ext_ref[b, 0:8, :] = xbc[L - 8:L, :]
        xbc = jax.nn.silu(conv)
        xs = xbc[:, 0:w]
        bm = xbc[:, w:2 * w]
        cm = xbc[:, 2 * w:3 * w]

        dt = jax.nn.softplus(sm_ref[b] + dtb_ref[...])
        d_a = dt * (-jnp.exp(alog_ref[0:1, :]) * alog_ref[1:2, :])
        cs = _cumsum_rows(d_a)
        cs_t = cs.T
        dt_full = _expand_heads(dt, SM_DT, w)
        cs_full = _expand_heads(cs, SM_DT, w)
        cs_last = cs_full[L - 1:L, :]
        xdt = xs * dt_full

        h_prev = h_ref[b]
        bm_t = bm.T
        y = jnp.zeros((L, w), f32)
        y_off = []
        st = []
        xw = xdt * jnp.exp(cs_last - cs_full)
        for grp in range(2):
            lo, hi = grp * SSD_DSTATE, (grp + 1) * SSD_DSTATE
            cb_g = _dot_nt(cm[:, lo:hi], bm[:, lo:hi])
            for r in range(2):
                hd = grp * 2 + r
                seg = cs[:, SM_DT + hd:SM_DT + hd + 1] - cs_t[SM_DT + hd:SM_DT + hd + 1, :]
                m = cb_g * jnp.where(causal, jnp.exp(seg), 0.0)
                y = y + _dot(m, jnp.where(lane_head == hd, xdt, 0.0))
            y_off.append(_dot(cm[:, lo:hi], h_prev[:, lo:hi]))
            st.append(_dot(bm_t[lo:hi, :], xw[:, lo:hi]))
        y = y + jnp.concatenate(y_off, axis=1) * jnp.exp(cs_full)
        h_ref[b] = h_prev * jnp.exp(cs_last) + jnp.concatenate(st, axis=1)
        y = y + dsk_ref[...] * xs
        y = y * jax.nn.silu(z)
        outs = []
        for grp in range(2):
            yg = y[:, grp * 128:(grp + 1) * 128]
            ms = jnp.mean(yg * yg, axis=-1, keepdims=True)
            outs.append(yg * lax.rsqrt(ms + GROUP_NORM_EPS))
        o_ref[b] = jnp.concatenate(outs, axis=1) * nw_ref[...]


def _ssd(p_ssd, p_sm, cw, cb, dtb_row, a_row, dsk, nw):
    nb, s, _ = p_ssd.shape
    L = SSD_CHUNK
    w = W_GROUP
    full = lambda a: pl.BlockSpec(a.shape, lambda c: (0,) * a.ndim)
    return pl.pallas_call(
        functools.partial(_ssd_body, nb=nb),
        grid=(s // L,),
        in_specs=[pl.BlockSpec((nb, L, 4 * w), lambda c: (0, c, 0)),
                  pl.BlockSpec((nb, L, SM_W), lambda c: (0, c, 0))]
        + [full(a) for a in (cw, cb, dtb_row, a_row, dsk, nw)],
        out_specs=pl.BlockSpec((nb, L, w), lambda c: (0, c, 0)),
        out_shape=jax.ShapeDtypeStruct((nb, s, w), f32),
        scratch_shapes=[pltpu.VMEM((nb, L + 8, 3 * w), f32), pltpu.VMEM((nb, SSD_DSTATE, w), f32)],
        compiler_params=pltpu.CompilerParams(dimension_semantics=("arbitrary",)),
        name="ssd",
    )(p_ssd, p_sm, cw, cb, dtb_row, a_row, dsk, nw)


def _level_masks():
    n = 4 * RW_CHUNK
    t = np.arange(n)[:, None]
    u = np.arange(n)[None, :]
    out = []
    s = 1
    while s < RW_CHUNK:
        out.append(((t // (2 * s)) == (u // (2 * s))) & ((t % (2 * s)) >= s) & ((u % (2 * s)) < s))
        s *= 2
    return jnp.asarray(np.stack(out).astype(np.float32), dtype=bf16)


def _ones_blocks(seg):
    i = np.arange(W_GROUP)
    return jnp.asarray((i[:, None] // seg == i[None, :] // seg).astype(np.float32), dtype=bf16)


def _pad_rows(m, row0, rows):
    return jnp.zeros((rows, m.shape[1]), m.dtype).at[row0:row0 + m.shape[0], :].set(m)


def _lane_row(v, lane0, width):
    return jnp.zeros((1, width), f32).at[0, lane0:lane0 + v.shape[0]].set(v)


def _block_diag(w):
    n, k, j = w.shape
    out = jnp.zeros((n * k, n * j), w.dtype)
    for i in range(n):
        out = out.at[i * k:(i + 1) * k, i * j:(i + 1) * j].set(w[i])
    return out


def kernel(x, ffn1_norm, ffn1_w_gate, ffn1_w_up, ffn1_w_down, mix_norm, w_in, w_out, gla_alpha_up, gla_alpha_bias, gla_norm, lru_conv_w, lru_conv_b, lru_w_a, lru_b_a, lru_w_x, lru_b_x, lru_lambda, rw_mu, rw_w0, rw_w2, rw_a0, rw_a2, rw_g2, rw_v0, rw_v1, rw_v2, rw_k_k, rw_k_a, rw_r_k, rw_gn_w, rw_gn_b, ssd_conv_w, ssd_conv_b, ssd_dt_bias, ssd_a_log, ssd_d, ssd_norm, ffn2_norm, ffn2_w_gate, ffn2_w_up, ffn2_w_down, final_norm):
    nb, s, d = x.shape
    t = nb * s
    depth = w_in.shape[0]
    tm = min(512, t)
    tm_in = min(256, t)
    tf = D_FF // 2

    wg1, wu1, wd1 = (a.astype(bf16) for a in (ffn1_w_gate, ffn1_w_up, ffn1_w_down))
    wg2, wu2, wd2 = (a.astype(bf16) for a in (ffn2_w_gate, ffn2_w_up, ffn2_w_down))
    w_out_b = w_out.astype(bf16)
    pad = jnp.zeros((depth, d, SM_W - 84), w_in.dtype)
    w_in_p = jnp.concatenate([
        w_in[:, :, _GLA0:_GLA0 + 768], w_in[:, :, _LRU0:_LRU0 + 512], w_in[:, :, _RW0:_RW0 + 768],
        w_in[:, :, _SSD0:_SSD0 + 1024],
        w_in[:, :, _GLA0 + 768:_GLA0 + 784], w_in[:, :, _RW0 + 768:_RW0 + 832],
        w_in[:, :, _SSD0 + 1024:_SSD0 + 1028], pad], axis=-1).astype(bf16)
    ones64 = _ones_blocks(64)
    lvl = _level_masks()

    x2 = x.reshape(t, d)
    v_first = None
    for l in range(depth):
        x2 = _ffn(x2, ffn1_norm[l][None, :], wg1, wu1, wd1, l, tm=tm, tf=tf)
        p_gla, p_lru, p_rw, p_ssd, p_sm = _inproj(x2, mix_norm[l][None, :], w_in_p, l, tm=tm_in)
        p_gla, p_lru, p_rw, p_ssd, p_sm = (a.reshape(nb, s, a.shape[-1]) for a in (p_gla, p_lru, p_rw, p_ssd, p_sm))

        y_gla = _gla(p_gla, p_sm, _pad_rows(gla_alpha_up[l], SM_GLA, SM_W), gla_alpha_bias[l][None, :],
                     jnp.tile(gla_norm[l], GLA_HEADS)[None, :], ones64)

        wax = jnp.concatenate([_block_diag(lru_w_a[l]), _block_diag(lru_w_x[l])], axis=1).astype(bf16)
        bax = jnp.concatenate([lru_b_a[l], lru_b_x[l]])[None, :]
        y_lru = _lru(p_lru, lru_conv_w[l], lru_conv_b[l][None, :], wax, bax,
                     lru_lambda[l][None, :])

        vecs = jnp.stack([rw_w0[l], rw_a0[l], rw_k_k[l], rw_k_a[l], rw_r_k[l].reshape(-1), rw_gn_w[l],
                          rw_gn_b[l], rw_v0[l - 1] if l > 0 else jnp.zeros_like(rw_w0[l])])
        mu = rw_mu[l]
        rw_args = (mu[None, 0:768], _lane_row(mu[768:832], SM_RW, SM_W),
                   _pad_rows(rw_w2[l], SM_RW, SM_W), _pad_rows(rw_a2[l], SM_RW + 16, SM_W),
                   _pad_rows(rw_g2[l], SM_RW + 32, SM_W), vecs)
        if l == 0:
            y_rw, v_first = _rwkv(p_rw, p_sm, None, *rw_args, None, None, ones64, lvl)
        else:
            v1p = jnp.zeros((W_GROUP, 128), f32).at[:, 0:rw_v1.shape[-1]].set(rw_v1[l - 1])
            v2p = _pad_rows(rw_v2[l - 1], 0, 128)
            y_rw = _rwkv(p_rw, p_sm, v_first, *rw_args, v1p, v2p, ones64, lvl)

        y_ssd = _ssd(p_ssd, p_sm, ssd_conv_w[l], ssd_conv_b[l][None, :],
                     _lane_row(ssd_dt_bias[l], SM_DT, SM_W), jnp.concatenate([_lane_row(ssd_a_log[l], SM_DT, SM_W),
                                      _lane_row(jnp.ones((SSD_HEADS,), f32), SM_DT, SM_W)]),
                     jnp.repeat(ssd_d[l], W_GROUP // SSD_HEADS)[None, :], ssd_norm[l][None, :])

        ys = [a.reshape(t, W_GROUP) for a in (y_gla, y_lru, y_rw, y_ssd)]
        x2 = _outproj(x2, ys, w_out_b, l, tm=tm)
        x2 = _ffn(x2, ffn2_norm[l][None, :], wg2, wu2, wd2, l, tm=tm, tf=tf)
    return _final_norm(x2, final_norm[None, :], tm=tm).reshape(nb, s, d)
```

```python
import functools
import math

import numpy as np
import jax
import jax.numpy as jnp
from jax import lax
from jax.experimental import pallas as pl
from jax.experimental.pallas import tpu as pltpu

f32 = jnp.float32
bf16 = jnp.bfloat16

D_MODEL = 1024
DEPTH = 4
D_FF = 2816
W_GROUP = 256
NORM_EPS = 1e-6
GROUP_NORM_EPS = 1e-5
CONV_W = 4

GLA_HEADS = 4
GLA_DK = 128
GLA_HK = 32
GLA_HV = 64
GLA_RANK = 16
GLA_GATE_NORM = 16.0
GLA_CHUNK = 64
GLA_TILE = 256

LRU_C = 8.0

RW_HS = 64
RW_DECAY_SCALE = math.exp(-0.5)
RW_GN_EPS = 64e-5
RW_CHUNK = 64
RW_TILE = 256

SSD_HEADS = 4
SSD_DSTATE = 128
SSD_CHUNK = 128

_GLA0, _LRU0, _RW0, _SSD0, _N_IN = 0, 784, 1296, 2128, 3156
SM_GLA = 0
SM_RW = 16
SM_DT = 80
SM_W = 128

LRU_TILE = 256
FFN_SPLIT = 2
VMEM_LIMIT = 56 * 1024 * 1024

def _dot(a, b):
    return jnp.dot(a.astype(bf16), b.astype(bf16), preferred_element_type=f32)


def _dot_nt(a, b):
    return lax.dot_general(a.astype(bf16), b.astype(bf16), (((1,), (1,)), ((), ())),
                           preferred_element_type=f32)


def _dot_tn(a, b):
    return lax.dot_general(a.astype(bf16), b.astype(bf16), (((0,), (0,)), ((), ())),
                           preferred_element_type=f32)


def _seg_dot(x, ones_bf):
    hi = x.astype(bf16)
    lo = (x - hi.astype(f32)).astype(bf16)
    return (jnp.dot(hi, ones_bf, preferred_element_type=f32)
            + jnp.dot(lo, ones_bf, preferred_element_type=f32))


def _cumsum_rows(x, seg=None):
    seg = x.shape[0] if seg is None else seg
    row = lax.broadcasted_iota(jnp.int32, x.shape, 0) % seg
    s = 1
    while s < seg:
        x = x + jnp.where(row >= s, pltpu.roll(x, s, 0), 0.0)
        s *= 2
    return x


def _head_masks(rows, width):
    lane = lax.broadcasted_iota(jnp.int32, (rows, width), 1) // (width // 4)
    return [(lane == h).astype(f32).astype(bf16) for h in range(4)]


def _stack_heads(x, masks):
    xb = x.astype(bf16)
    return jnp.concatenate([xb * m for m in masks], axis=0)


def _expand_heads(x, lane0, width):
    n = x.shape[0]
    lane = lax.broadcasted_iota(jnp.int32, (n, width), 1) // (width // 4)
    out = jnp.zeros((n, width), f32)
    for h in range(4):
        out = jnp.where(lane == h, x[:, lane0 + h:lane0 + h + 1], out)
    return out


def _rms_norm(x, w):
    ms = jnp.mean(x * x, axis=-1, keepdims=True)
    return x * lax.rsqrt(ms + NORM_EPS) * w


def _ffn_body(*refs, n_split, proj, final):
    refs = list(refs)
    x_ref = refs.pop(0)
    if proj:
        y_refs = [refs.pop(0) for _ in range(4)]
        wo_ref = refs.pop(0)
    nw_ref, wg_ref, wu_ref, wd_ref = (refs.pop(0) for _ in range(4))
    if final:
        fw_ref = refs.pop(0)
    o_ref = refs.pop(0)

    x = x_ref[...]
    if proj:
        for i, y_ref in enumerate(y_refs):
            x = x + jnp.dot(y_ref[...].astype(bf16), wo_ref[i * W_GROUP:(i + 1) * W_GROUP, :],
                            preferred_element_type=f32)
    xn = _rms_norm(x, nw_ref[...]).astype(bf16)
    tf = wg_ref.shape[-1] // n_split
    acc = None
    for s in range(n_split):
        g = jnp.dot(xn, wg_ref[:, s * tf:(s + 1) * tf], preferred_element_type=f32)
        u = jnp.dot(xn, wu_ref[:, s * tf:(s + 1) * tf], preferred_element_type=f32)
        h = (jax.nn.silu(g) * u).astype(bf16)
        part = jnp.dot(h, wd_ref[s * tf:(s + 1) * tf, :], preferred_element_type=f32)
        acc = part if acc is None else acc + part
    out = x + 0.5 * acc
    if final:
        out = _rms_norm(out, fw_ref[...])
    o_ref[...] = out


def _ffn(x2, nw, wg, wu, wd, layer, *, tm, ys=None, w_out=None, final_w=None):
    t, d = x2.shape
    f = wg.shape[-1]
    proj = ys is not None
    final = final_w is not None
    tile = lambda width: pl.BlockSpec((tm, width), lambda i: (i, 0))
    row = pl.BlockSpec((1, d), lambda i: (0, 0))
    resident = lambda r, c: pl.BlockSpec((None, r, c), lambda i: (layer, 0, 0), pipeline_mode=pl.Buffered(1))
    ins = [x2] + (list(ys) + [w_out] if proj else []) + [nw, wg, wu, wd] + ([final_w] if final else [])
    in_specs = [tile(d)] + ([tile(W_GROUP)] * 4 + [resident(d, d)] if proj else []) \
        + [row, resident(d, f), resident(d, f), resident(f, d)] + ([row] if final else [])
    return pl.pallas_call(
        functools.partial(_ffn_body, n_split=FFN_SPLIT, proj=proj, final=final),
        grid=(t // tm,),
        in_specs=in_specs,
        out_specs=tile(d),
        out_shape=jax.ShapeDtypeStruct((t, d), f32),
        compiler_params=pltpu.CompilerParams(
            dimension_semantics=("parallel",), vmem_limit_bytes=VMEM_LIMIT),
        name="ffn",
    )(*ins)


_SEG_W = (768, 512, 768, 1024, SM_W)
_SEG_O = tuple(int(o) for o in np.cumsum((0,) + _SEG_W))


def _inproj_body(x_ref, nw_ref, w_ref, o_gla, o_lru, o_rw, o_ssd, o_sm):
    x = x_ref[...]
    ms = jnp.mean(x * x, axis=-1, keepdims=True)
    xn = (x * lax.rsqrt(ms + NORM_EPS) * nw_ref[...]).astype(bf16)
    for o_ref, lo, hi in zip((o_gla, o_lru, o_rw, o_ssd, o_sm), _SEG_O[:-1], _SEG_O[1:]):
        o_ref[...] = jnp.dot(xn, w_ref[:, lo:hi], preferred_element_type=f32)


def _inproj(x2, nw, w_in_p, layer, *, tm):
    t, d = x2.shape
    n = w_in_p.shape[-1]
    return pl.pallas_call(
        _inproj_body,
        grid=(t // tm,),
        in_specs=[
            pl.BlockSpec((tm, d), lambda i: (i, 0)),
            pl.BlockSpec((1, d), lambda i: (0, 0)),
            pl.BlockSpec((None, d, n), lambda i: (layer, 0, 0)),
        ],
        out_specs=[pl.BlockSpec((tm, w), lambda i: (i, 0)) for w in _SEG_W],
        out_shape=[jax.ShapeDtypeStruct((t, w), f32) for w in _SEG_W],
        compiler_params=pltpu.CompilerParams(
            dimension_semantics=("parallel",), vmem_limit_bytes=VMEM_LIMIT),
        name="inproj",
    )(x2, nw, w_in_p)


def _gla_body(p_ref, sm_ref, wup_ref, bias_ref, nw_ref, ones_ref, o_ref, st_ref, *, nb):
    c = pl.program_id(0)
    L = GLA_CHUNK
    tc = GLA_TILE
    nch = tc // L

    @pl.when(c == 0)
    def _():
        st_ref[...] = jnp.zeros_like(st_ref)

    row = lax.broadcasted_iota(jnp.int32, (L, 4 * L), 0)
    col = lax.broadcasted_iota(jnp.int32, (L, 4 * L), 1)
    causal = (col % L) <= row
    srow = lax.broadcasted_iota(jnp.int32, (W_GROUP, GLA_DK), 0) // GLA_HV
    scol = lax.broadcasted_iota(jnp.int32, (W_GROUP, GLA_DK), 1) // GLA_HK
    blockdiag = srow == scol
    k_masks = _head_masks(L, GLA_DK)
    v_masks = _head_masks(L, W_GROUP)

    pre = []
    for b in range(nb):
        x = p_ref[b]
        z = _dot(sm_ref[b], wup_ref[...]) + bias_ref[...]
        bc = _cumsum_rows(jax.nn.log_sigmoid(z) / GLA_GATE_NORM, L)
        q_dec = (x[:, 0:128] * (GLA_HK ** -0.5) * jnp.exp(bc)).astype(bf16)
        pre.append((x, bc, q_dec, x[:, 128:256] * jnp.exp(-bc)))
    chains = [(j, b) for j in range(nch) for b in range(nb)]
    intra, kv, dec = {}, {}, {}
    for j, b in chains:
        x, bc, q_dec, k_dec = pre[b]
        rows = slice(j * L, (j + 1) * L)
        s_cat = _dot_nt(q_dec[rows], _stack_heads(k_dec[rows], k_masks))
        s_cat = jnp.where(causal, s_cat, 0.0)
        v = x[rows, 256:512]
        intra[j, b] = _dot(s_cat, _stack_heads(v, v_masks))
        b_last = bc[(j + 1) * L - 1:(j + 1) * L, :]
        k_end = x[rows, 128:256] * jnp.exp(b_last - bc[rows])
        kv[j, b] = jnp.where(blockdiag, _dot_tn(v, k_end), 0.0)
        dec[j, b] = jnp.exp(b_last)
    st = [st_ref[b] for b in range(nb)]
    outs = [[None] * nch for _ in range(nb)]
    for j, b in chains:
        q_dec = pre[b][2]
        outs[b][j] = intra[j, b] + _dot_nt(q_dec[j * L:(j + 1) * L], st[b])
        st[b] = st[b] * dec[j, b] + kv[j, b]
    for b in range(nb):
        st_ref[b] = st[b]
        o = jnp.concatenate(outs[b], axis=0)
        ms = _seg_dot(o * o, ones_ref[...]) * (1.0 / GLA_HV)
        o = o * lax.rsqrt(ms + GROUP_NORM_EPS) * nw_ref[...]
        o_ref[b] = o * jax.nn.silu(pre[b][0][:, 512:768])


def _gla(p_gla, p_sm, wup_pad, bias, nw, ones64):
    nb, s, _ = p_gla.shape
    tc = GLA_TILE
    return pl.pallas_call(
        functools.partial(_gla_body, nb=nb),
        grid=(s // tc,),
        in_specs=[
            pl.BlockSpec((nb, tc, 768), lambda c: (0, c, 0)),
            pl.BlockSpec((nb, tc, SM_W), lambda c: (0, c, 0)),
            pl.BlockSpec((SM_W, GLA_DK), lambda c: (0, 0)),
            pl.BlockSpec((1, GLA_DK), lambda c: (0, 0)),
            pl.BlockSpec((1, W_GROUP), lambda c: (0, 0)),
            pl.BlockSpec((W_GROUP, W_GROUP), lambda c: (0, 0)),
        ],
        out_specs=pl.BlockSpec((nb, tc, W_GROUP), lambda c: (0, c, 0)),
        out_shape=jax.ShapeDtypeStruct((nb, s, W_GROUP), f32),
        scratch_shapes=[pltpu.VMEM((nb, W_GROUP, GLA_DK), f32)],
        compiler_params=pltpu.CompilerParams(dimension_semantics=("arbitrary",)),
        name="gla",
    )(p_gla, p_sm, wup_pad, bias, nw, ones64)


def _lru_body(p_ref, cw_ref, cb_ref, wax_ref, bax_ref, lam_ref, o_ref, ext_ref, h_ref, *, nb):
    c = pl.program_id(0)
    tc = LRU_TILE
    w = W_GROUP

    @pl.when(c == 0)
    def _():
        for b in range(nb):
            ext_ref[b, 0:8, :] = jnp.zeros((8, w), f32)
        h_ref[...] = jnp.zeros_like(h_ref)

    row = lax.broadcasted_iota(jnp.int32, (tc, w), 0)
    for b in range(nb):
        x = p_ref[b]
        xb = x[:, 0:w]
        gate = x[:, w:2 * w]
        ext_ref[b, 8:, :] = xb
        xc = cb_ref[...] + cw_ref[CONV_W - 1:CONV_W, :] * xb
        for j in range(1, CONV_W):
            xc = xc + cw_ref[CONV_W - 1 - j:CONV_W - j, :] * ext_ref[b, pl.ds(8 - j, tc), :]
        ext_ref[b, 0:8, :] = xb[tc - 8:tc, :]
        ri = jax.nn.sigmoid(_dot(xc, wax_ref[...]) + bax_ref[...])
        r = ri[:, 0:w]
        i = ri[:, w:2 * w]
        log_a = -LRU_C * r * jax.nn.softplus(-lam_ref[...])
        a = jnp.exp(log_a)
        u = jnp.sqrt(1.0 - a * a) * (i * xc)
        s = 1
        while s < tc:
            keep = row >= s
            u = u + a * jnp.where(keep, pltpu.roll(u, s, 0), 0.0)
            a = a * jnp.where(keep, pltpu.roll(a, s, 0), 1.0)
            s *= 2
        h = u + a * h_ref[b]
        h_ref[b] = h[tc - 1:tc, :]
        o_ref[b] = h * jax.nn.gelu(gate)


def _lru(p_lru, cw, cb, wax, bax, sp):
    nb, s, _ = p_lru.shape
    tc = LRU_TILE
    w = W_GROUP
    return pl.pallas_call(
        functools.partial(_lru_body, nb=nb),
        grid=(s // tc,),
        in_specs=[
            pl.BlockSpec((nb, tc, 2 * w), lambda c: (0, c, 0)),
            pl.BlockSpec((CONV_W, w), lambda c: (0, 0)),
            pl.BlockSpec((1, w), lambda c: (0, 0)),
            pl.BlockSpec((w, 2 * w), lambda c: (0, 0)),
            pl.BlockSpec((1, 2 * w), lambda c: (0, 0)),
            pl.BlockSpec((1, w), lambda c: (0, 0)),
        ],
        out_specs=pl.BlockSpec((nb, tc, w), lambda c: (0, c, 0)),
        out_shape=jax.ShapeDtypeStruct((nb, s, w), f32),
        scratch_shapes=[pltpu.VMEM((nb, tc + 8, w), f32), pltpu.VMEM((nb, 1, w), f32)],
        compiler_params=pltpu.CompilerParams(dimension_semantics=("arbitrary",)),
        name="rglru",
    )(p_lru, cw, cb, wax, bax, sp)


def _rw_body(*refs, nb, mix):
    if mix:
        (p_ref, sm_ref, vf_ref, mu_ref, mus_ref, w2_ref, a2_ref, g2_ref, vec_ref, v1_ref, v2_ref,
         ones_ref, lvl_ref, o_ref, cx_ref, cs_ref, ht_ref) = refs
    else:
        (p_ref, sm_ref, mu_ref, mus_ref, w2_ref, a2_ref, g2_ref, vec_ref,
         ones_ref, lvl_ref, o_ref, vo_ref, cx_ref, cs_ref, ht_ref) = refs
    c = pl.program_id(0)
    L = RW_CHUNK
    tc = RW_TILE
    nch = tc // L
    w = W_GROUP
    n4 = 4 * L

    @pl.when(c == 0)
    def _():
        cx_ref[...] = jnp.zeros_like(cx_ref)
        cs_ref[...] = jnp.zeros_like(cs_ref)
        ht_ref[...] = jnp.zeros_like(ht_ref)

    row3 = lax.broadcasted_iota(jnp.int32, (tc, 3 * w), 0)
    rows_sm = lax.broadcasted_iota(jnp.int32, (tc, SM_W), 0)
    rr = lax.broadcasted_iota(jnp.int32, (n4, n4), 0)
    cc = lax.broadcasted_iota(jnp.int32, (n4, n4), 1)
    strict = (rr % L) > (cc % L)
    incl = (rr % L) >= (cc % L)
    eye = (rr == cc).astype(f32).astype(bf16)
    masks = _head_masks(L, w)
    w0, a0, k_k, k_a, r_k, gn_w, gn_b, v0 = (vec_ref[i:i + 1, :] for i in range(8))
    ones_bf = ones_ref[...]

    pre = []
    for b in range(nb):
        x = p_ref[b]
        sm = sm_ref[b]
        x_prev = jnp.where(row3 == 0, cx_ref[b], pltpu.roll(x, 1, 0))
        sm_prev = jnp.where(rows_sm == 0, cs_ref[b], pltpu.roll(sm, 1, 0))
        cx_ref[b] = x[tc - 1:tc, :]
        cs_ref[b] = sm[tc - 1:tc, :]
        x = x + (x_prev - x) * mu_ref[...]
        sm = sm + (sm_prev - sm) * mus_ref[...]
        r = x[:, 0:w]
        k = x[:, w:2 * w]
        v = x[:, 2 * w:3 * w]
        log_w = -RW_DECAY_SCALE * jax.nn.sigmoid(w0 + _dot(jnp.tanh(sm), w2_ref[...]))
        a = jax.nn.sigmoid(a0 + _dot(sm, a2_ref[...]))
        g = _dot(jax.nn.sigmoid(sm), g2_ref[...])
        if mix:
            lam = jax.nn.sigmoid(v0 + _dot(_dot(v, v1_ref[...]), v2_ref[...]))
            v = v + (vf_ref[b] - v) * lam
        else:
            vo_ref[b] = v
        kk = k * k_k
        n2 = _seg_dot(kk * kk, ones_bf)
        kk = kk / jnp.maximum(jnp.sqrt(n2), 1e-12)
        k = k * (1.0 + (a - 1.0) * k_a)
        cum = _cumsum_rows(log_w, L)
        g_in = jnp.exp(cum)
        g_inv = jnp.exp(-cum)
        pre.append(dict(r=r, k=k, v=v, g=g, g_in=g_in,
                        al=-kk * jnp.exp(cum - log_w),
                        be=kk * a * g_inv, kt=k * g_inv, rt=r * g_in))

    chains = [(j, b) for j in range(nch) for b in range(nb)]
    ch = {}
    for j, b in chains:
        p = pre[b]
        rows = slice(j * L, (j + 1) * L)
        g_last = p["g_in"][(j + 1) * L - 1:(j + 1) * L, :]
        d = dict(g_last=g_last)
        d["al"] = _stack_heads(p["al"][rows], masks)
        d["rt"] = _stack_heads(p["rt"][rows], masks)
        d["vst"] = _stack_heads(p["v"][rows], masks)
        be = _stack_heads(p["be"][rows], masks)
        kt = _stack_heads(p["kt"][rows], masks)
        d["be_end"] = _stack_heads(p["be"][rows] * g_last, masks)
        d["kt_end"] = _stack_heads(p["kt"][rows] * g_last, masks)
        aa = _dot_nt(jnp.concatenate([d["al"], d["rt"]], axis=0), jnp.concatenate([be, kt], axis=0))
        d["a_ab"] = jnp.where(strict, aa[0:n4, 0:n4], 0.0).astype(bf16)
        d["a_ak"] = jnp.where(strict, aa[0:n4, n4:2 * n4], 0.0).astype(bf16)
        d["a_rb"] = jnp.where(incl, aa[n4:2 * n4, 0:n4], 0.0).astype(bf16)
        d["a_rk"] = jnp.where(incl, aa[n4:2 * n4, n4:2 * n4], 0.0).astype(bf16)
        ch[j, b] = d

    for key in chains:
        ch[key]["t"] = eye + ch[key]["a_ab"] * lvl_ref[0]
    for lv in range(1, 6):
        x1 = {key: _dot(ch[key]["t"], ch[key]["a_ab"] * lvl_ref[lv]) for key in chains}
        for key in chains:
            ch[key]["t"] = ch[key]["t"] + _dot(x1[key], ch[key]["t"]).astype(bf16)

    for key in chains:
        d = ch[key]
        akv = _dot(d["a_ak"], d["vst"]).astype(bf16)
        pq = _dot(d["t"], jnp.concatenate([d["al"], akv], axis=1))
        d["p_st"] = pq[:, 0:w].astype(bf16)
        d["q_st"] = pq[:, w:2 * w]
        d["y_k"] = _dot(d["a_rk"], d["vst"])
        d["kv_t"] = _dot_tn(d["vst"], d["kt_end"])

    ht = [ht_ref[b] for b in range(nb)]
    ys = [[None] * nch for _ in range(nb)]
    for j, b in chains:
        d = ch[j, b]
        ph = _dot_nt(jnp.concatenate([d["p_st"], d["rt"]], axis=0), ht[b])
        u = (ph[0:n4] + d["q_st"]).astype(bf16)
        y_st = ph[n4:2 * n4] + _dot(d["a_rb"], u) + d["y_k"]
        ht[b] = ht[b] * d["g_last"] + _dot_tn(u, d["be_end"]) + d["kv_t"]
        ys[b][j] = y_st[0:L] + y_st[L:2 * L] + y_st[2 * L:3 * L] + y_st[3 * L:4 * L]

    for b in range(nb):
        ht_ref[b] = ht[b]
        p = pre[b]
        y = jnp.concatenate(ys[b], axis=0)
        mean = _seg_dot(y, ones_bf) * (1.0 / RW_HS)
        dlt = y - mean
        var = _seg_dot(dlt * dlt, ones_bf) * (1.0 / RW_HS)
        yn = dlt * lax.rsqrt(var + RW_GN_EPS) * gn_w + gn_b
        bonus = _seg_dot(p["r"] * p["k"] * r_k, ones_bf) * p["v"]
        o_ref[b] = (yn + bonus) * p["g"]


def _rwkv(p_rw, p_sm, v_first, mu, mus, w2p, a2p, g2p, vecs, v1p, v2p, ones64, lvl):
    nb, s, _ = p_rw.shape
    tc = RW_TILE
    w = W_GROUP
    mix = v_first is not None
    blk = lambda width: pl.BlockSpec((nb, tc, width), lambda c: (0, c, 0))
    full = lambda a: pl.BlockSpec(a.shape, lambda c: (0,) * a.ndim)
    ins = [p_rw, p_sm] + ([v_first] if mix else []) + [mu, mus, w2p, a2p, g2p, vecs] \
        + ([v1p, v2p] if mix else []) + [ones64, lvl]
    in_specs = [blk(3 * w), blk(SM_W)] + ([blk(w)] if mix else []) + [full(a) for a in ins[(3 if mix else 2):]]
    out_shape = jax.ShapeDtypeStruct((nb, s, w), f32)
    return pl.pallas_call(
        functools.partial(_rw_body, nb=nb, mix=mix),
        grid=(s // tc,),
        in_specs=in_specs,
        out_specs=blk(w) if mix else [blk(w), blk(w)],
        out_shape=out_shape if mix else [out_shape, out_shape],
        scratch_shapes=[pltpu.VMEM((nb, 1, 3 * w), f32), pltpu.VMEM((nb, 1, SM_W), f32),
                        pltpu.VMEM((nb, w, w), f32)],
        compiler_params=pltpu.CompilerParams(
            dimension_semantics=("arbitrary",), vmem_limit_bytes=VMEM_LIMIT),
        name="rwkv7",
    )(*ins)


def _ssd_body(p_ref, sm_ref, cw_ref, cb_ref, dtb_ref, alog_ref, dsk_ref, nw_ref, o_ref, ext_ref, h_ref, *, nb):
    c = pl.program_id(0)
    L = SSD_CHUNK
    w = W_GROUP
    cdim = 3 * w

    @pl.when(c == 0)
    def _():
        for b in range(nb):
            ext_ref[b, 0:8, :] = jnp.zeros((8, cdim), f32)
        h_ref[...] = jnp.zeros_like(h_ref)

    ll = lax.broadcasted_iota(jnp.int32, (L, L), 0)
    ss = lax.broadcasted_iota(jnp.int32, (L, L), 1)
    causal = ll >= ss
    lane_head = lax.broadcasted_iota(jnp.int32, (L, w), 1) // (w // SSD_HEADS)

    for b in range(nb):
        x = p_ref[b]
        z = x[:, 0:w]
        xbc = x[:, w:4 * w]
        ext_ref[b, 8:, :] = xbc
        conv = cb_ref[...] + cw_ref[CONV_W - 1:CONV_W, :] * xbc
        for j in range(1, CONV_W):
            conv = conv + cw_ref[CONV_W - 1 - j:CONV_W - j, :] * ext_ref[b, pl.ds(8 - j, L), :]
        ext_ref[b, 0:8, :] = xbc[L - 8:L, :]
        xbc = jax.nn.silu(conv)
        xs = xbc[:, 0:w]
        bm = xbc[:, w:2 * w]
        cm = xbc[:, 2 * w:3 * w]

        dt = jax.nn.softplus(sm_ref[b] + dtb_ref[...])
        d_a = dt * (-jnp.exp(alog_ref[0:1, :]) * alog_ref[1:2, :])
        cs = _cumsum_rows(d_a)
        cs_t = cs.T
        dt_full = _expand_heads(dt, SM_DT, w)
        cs_full = _expand_heads(cs, SM_DT, w)
        cs_last = cs_full[L - 1:L, :]
        xdt = xs * dt_full

        h_prev = h_ref[b]
        bm_t = bm.T
        y = jnp.zeros((L, w), f32)
        y_off = []
        st = []
        xw = xdt * jnp.exp(cs_last - cs_full)
        for grp in range(2):
            lo, hi = grp * SSD_DSTATE, (grp + 1) * SSD_DSTATE
            cb_g = _dot_nt(cm[:, lo:hi], bm[:, lo:hi])
            for r in range(2):
                hd = grp * 2 + r
                seg = cs[:, SM_DT + hd:SM_DT + hd + 1] - cs_t[SM_DT + hd:SM_DT + hd + 1, :]
                m = cb_g * jnp.where(causal, jnp.exp(seg), 0.0)
                y = y + _dot(m, jnp.where(lane_head == hd, xdt, 0.0))
            y_off.append(_dot(cm[:, lo:hi], h_prev[:, lo:hi]))
            st.append(_dot(bm_t[lo:hi, :], xw[:, lo:hi]))
        y = y + jnp.concatenate(y_off, axis=1) * jnp.exp(cs_full)
        h_ref[b] = h_prev * jnp.exp(cs_last) + jnp.concatenate(st, axis=1)
        y = y + dsk_ref[...] * xs
        y = y * jax.nn.silu(z)
        outs = []
        for grp in range(2):
            yg = y[:, grp * 128:(grp + 1) * 128]
            ms = jnp.mean(yg * yg, axis=-1, keepdims=True)
            outs.append(yg * lax.rsqrt(ms + GROUP_NORM_EPS))
        o_ref[b] = jnp.concatenate(outs, axis=1) * nw_ref[...]


def _ssd(p_ssd, p_sm, cw, cb, dtb_row, a_row, dsk, nw):
    nb, s, _ = p_ssd.shape
    L = SSD_CHUNK
    w = W_GROUP
    full = lambda a: pl.BlockSpec(a.shape, lambda c: (0,) * a.ndim)
    return pl.pallas_call(
        functools.partial(_ssd_body, nb=nb),
        grid=(s // L,),
        in_specs=[pl.BlockSpec((nb, L, 4 * w), lambda c: (0, c, 0)),
                  pl.BlockSpec((nb, L, SM_W), lambda c: (0, c, 0))]
        + [full(a) for a in (cw, cb, dtb_row, a_row, dsk, nw)],
        out_specs=pl.BlockSpec((nb, L, w), lambda c: (0, c, 0)),
        out_shape=jax.ShapeDtypeStruct((nb, s, w), f32),
        scratch_shapes=[pltpu.VMEM((nb, L + 8, 3 * w), f32), pltpu.VMEM((nb, SSD_DSTATE, w), f32)],
        compiler_params=pltpu.CompilerParams(dimension_semantics=("arbitrary",)),
        name="ssd",
    )(p_ssd, p_sm, cw, cb, dtb_row, a_row, dsk, nw)


def _level_masks():
    n = 4 * RW_CHUNK
    t = np.arange(n)[:, None]
    u = np.arange(n)[None, :]
    out = []
    s = 1
    while s < RW_CHUNK:
        out.append(((t // (2 * s)) == (u // (2 * s))) & ((t % (2 * s)) >= s) & ((u % (2 * s)) < s))
        s *= 2
    return jnp.asarray(np.stack(out).astype(np.float32), dtype=bf16)


def _ones_blocks(seg):
    i = np.arange(W_GROUP)
    return jnp.asarray((i[:, None] // seg == i[None, :] // seg).astype(np.float32), dtype=bf16)


def _pad_rows(m, row0, rows):
    return jnp.zeros((rows, m.shape[1]), m.dtype).at[row0:row0 + m.shape[0], :].set(m)


def _lane_row(v, lane0, width):
    return jnp.zeros((1, width), f32).at[0, lane0:lane0 + v.shape[0]].set(v)


def _block_diag(w):
    n, k, j = w.shape
    out = jnp.zeros((n * k, n * j), w.dtype)
    for i in range(n):
        out = out.at[i * k:(i + 1) * k, i * j:(i + 1) * j].set(w[i])
    return out


def kernel(x, ffn1_norm, ffn1_w_gate, ffn1_w_up, ffn1_w_down, mix_norm, w_in, w_out, gla_alpha_up, gla_alpha_bias, gla_norm, lru_conv_w, lru_conv_b, lru_w_a, lru_b_a, lru_w_x, lru_b_x, lru_lambda, rw_mu, rw_w0, rw_w2, rw_a0, rw_a2, rw_g2, rw_v0, rw_v1, rw_v2, rw_k_k, rw_k_a, rw_r_k, rw_gn_w, rw_gn_b, ssd_conv_w, ssd_conv_b, ssd_dt_bias, ssd_a_log, ssd_d, ssd_norm, ffn2_norm, ffn2_w_gate, ffn2_w_up, ffn2_w_down, final_norm):
    nb, s, d = x.shape
    t = nb * s
    depth = w_in.shape[0]
    tm = min(512, t)
    tm_in = min(256, t)

    wg1, wu1, wd1 = (a.astype(bf16) for a in (ffn1_w_gate, ffn1_w_up, ffn1_w_down))
    wg2, wu2, wd2 = (a.astype(bf16) for a in (ffn2_w_gate, ffn2_w_up, ffn2_w_down))
    w_out_b = w_out.astype(bf16)
    pad = jnp.zeros((depth, d, SM_W - 84), w_in.dtype)
    w_in_p = jnp.concatenate([
        w_in[:, :, _GLA0:_GLA0 + 768], w_in[:, :, _LRU0:_LRU0 + 512], w_in[:, :, _RW0:_RW0 + 768],
        w_in[:, :, _SSD0:_SSD0 + 1024],
        w_in[:, :, _GLA0 + 768:_GLA0 + 784], w_in[:, :, _RW0 + 768:_RW0 + 832],
        w_in[:, :, _SSD0 + 1024:_SSD0 + 1028], pad], axis=-1).astype(bf16)
    ones64 = _ones_blocks(64)
    lvl = _level_masks()

    x2 = x.reshape(t, d)
    v_first = None
    for l in range(depth):
        x2 = _ffn(x2, ffn1_norm[l][None, :], wg1, wu1, wd1, l, tm=tm)
        p_gla, p_lru, p_rw, p_ssd, p_sm = _inproj(x2, mix_norm[l][None, :], w_in_p, l, tm=tm_in)
        p_gla, p_lru, p_rw, p_ssd, p_sm = (a.reshape(nb, s, a.shape[-1]) for a in (p_gla, p_lru, p_rw, p_ssd, p_sm))

        y_gla = _gla(p_gla, p_sm, _pad_rows(gla_alpha_up[l], SM_GLA, SM_W), gla_alpha_bias[l][None, :],
                     jnp.tile(gla_norm[l], GLA_HEADS)[None, :], ones64)

        wax = jnp.concatenate([_block_diag(lru_w_a[l]), _block_diag(lru_w_x[l])], axis=1).astype(bf16)
        bax = jnp.concatenate([lru_b_a[l], lru_b_x[l]])[None, :]
        y_lru = _lru(p_lru, lru_conv_w[l], lru_conv_b[l][None, :], wax, bax,
                     lru_lambda[l][None, :])

        vecs = jnp.stack([rw_w0[l], rw_a0[l], rw_k_k[l], rw_k_a[l], rw_r_k[l].reshape(-1), rw_gn_w[l],
                          rw_gn_b[l], rw_v0[l - 1] if l > 0 else jnp.zeros_like(rw_w0[l])])
        mu = rw_mu[l]
        rw_args = (mu[None, 0:768], _lane_row(mu[768:832], SM_RW, SM_W),
                   _pad_rows(rw_w2[l], SM_RW, SM_W), _pad_rows(rw_a2[l], SM_RW + 16, SM_W),
                   _pad_rows(rw_g2[l], SM_RW + 32, SM_W), vecs)
        if l == 0:
            y_rw, v_first = _rwkv(p_rw, p_sm, None, *rw_args, None, None, ones64, lvl)
        else:
            v1p = jnp.zeros((W_GROUP, 128), f32).at[:, 0:rw_v1.shape[-1]].set(rw_v1[l - 1])
            v2p = _pad_rows(rw_v2[l - 1], 0, 128)
            y_rw = _rwkv(p_rw, p_sm, v_first, *rw_args, v1p, v2p, ones64, lvl)

        y_ssd = _ssd(p_ssd, p_sm, ssd_conv_w[l], ssd_conv_b[l][None, :],
                     _lane_row(ssd_dt_bias[l], SM_DT, SM_W), jnp.concatenate([_lane_row(ssd_a_log[l], SM_DT, SM_W),
                                      _lane_row(jnp.ones((SSD_HEADS,), f32), SM_DT, SM_W)]),
                     jnp.repeat(ssd_d[l], W_GROUP // SSD_HEADS)[None, :], ssd_norm[l][None, :])

        ys = [a.reshape(t, W_GROUP) for a in (y_gla, y_lru, y_rw, y_ssd)]
        x2 = _ffn(x2, ffn2_norm[l][None, :], wg2, wu2, wd2, l, tm=tm, ys=ys, w_out=w_out_b,
                  final_w=final_norm[None, :] if l == depth - 1 else None)
    return x2.reshape(nb, s, d)
```

```python
import functools
import math

import numpy as np
import jax
import jax.numpy as jnp
from jax import lax
from jax.experimental import pallas as pl
from jax.experimental.pallas import tpu as pltpu

f32 = jnp.float32
bf16 = jnp.bfloat16

D_MODEL = 1024
DEPTH = 4
D_FF = 2816
W_GROUP = 256
NORM_EPS = 1e-6
GROUP_NORM_EPS = 1e-5
CONV_W = 4

GLA_HEADS = 4
GLA_DK = 128
GLA_HK = 32
GLA_HV = 64
GLA_RANK = 16
GLA_GATE_NORM = 16.0
GLA_CHUNK = 64
GLA_TILE = 256

LRU_C = 8.0

RW_HS = 64
RW_DECAY_SCALE = math.exp(-0.5)
RW_GN_EPS = 64e-5
RW_CHUNK = 64
RW_TILE = 256

SSD_HEADS = 4
SSD_DSTATE = 128
SSD_CHUNK = 128

_GLA0, _LRU0, _RW0, _SSD0, _N_IN = 0, 784, 1296, 2128, 3156
SM_GLA = 0
SM_RW = 16
SM_DT = 80
SM_W = 128

LRU_TILE = 256
MXU_TILE = 256
FFN_SPLITS = (MXU_TILE,) * (D_FF // MXU_TILE)
VMEM_LIMIT = 56 * 1024 * 1024

def _dot(a, b):
    return jnp.dot(a.astype(bf16), b.astype(bf16), preferred_element_type=f32)


def _dot_nt(a, b):
    return lax.dot_general(a.astype(bf16), b.astype(bf16), (((1,), (1,)), ((), ())),
                           preferred_element_type=f32)


def _dot_tn(a, b):
    return lax.dot_general(a.astype(bf16), b.astype(bf16), (((0,), (0,)), ((), ())),
                           preferred_element_type=f32)


def _seg_dot(x, ones_bf):
    hi = x.astype(bf16)
    lo = (x - hi.astype(f32)).astype(bf16)
    return (jnp.dot(hi, ones_bf, preferred_element_type=f32)
            + jnp.dot(lo, ones_bf, preferred_element_type=f32))


def _cumsum_rows(x, seg=None):
    seg = x.shape[0] if seg is None else seg
    row = lax.broadcasted_iota(jnp.int32, x.shape, 0) % seg
    s = 1
    while s < seg:
        x = x + jnp.where(row >= s, pltpu.roll(x, s, 0), 0.0)
        s *= 2
    return x


def _head_masks(rows, width):
    lane = lax.broadcasted_iota(jnp.int32, (rows, width), 1) // (width // 4)
    return [(lane == h).astype(f32).astype(bf16) for h in range(4)]


def _stack_heads(x, masks):
    xb = x.astype(bf16)
    return jnp.concatenate([xb * m for m in masks], axis=0)


def _expand_heads(x, lane0, width):
    n = x.shape[0]
    lane = lax.broadcasted_iota(jnp.int32, (n, width), 1) // (width // 4)
    out = jnp.zeros((n, width), f32)
    for h in range(4):
        out = jnp.where(lane == h, x[:, lane0 + h:lane0 + h + 1], out)
    return out


def _rms_norm(x, w):
    ms = jnp.mean(x * x, axis=-1, keepdims=True)
    return x * lax.rsqrt(ms + NORM_EPS) * w


def _ffn_body(*refs, splits, proj, final):
    refs = list(refs)
    x_ref = refs.pop(0)
    if proj:
        y_refs = [refs.pop(0) for _ in range(4)]
        wo_ref = refs.pop(0)
    nw_ref, wg_ref, wu_ref, wd_ref = (refs.pop(0) for _ in range(4))
    if final:
        fw_ref = refs.pop(0)
    o_ref = refs.pop(0)

    x = x_ref[...]
    if proj:
        for i, y_ref in enumerate(y_refs):
            x = x + jnp.dot(y_ref[...].astype(bf16), wo_ref[i * W_GROUP:(i + 1) * W_GROUP, :],
                            preferred_element_type=f32)
    xn = _rms_norm(x, nw_ref[...]).astype(bf16)
    acc = None
    lo = 0
    for width in splits:
        hi = lo + width
        g = jnp.dot(xn, wg_ref[:, lo:hi], preferred_element_type=f32)
        u = jnp.dot(xn, wu_ref[:, lo:hi], preferred_element_type=f32)
        h = (jax.nn.silu(g) * u).astype(bf16)
        part = jnp.dot(h, wd_ref[lo:hi, :], preferred_element_type=f32)
        acc = part if acc is None else acc + part
        lo = hi
    out = x + 0.5 * acc
    if final:
        out = _rms_norm(out, fw_ref[...])
    o_ref[...] = out


def _ffn(x2, nw, wg, wu, wd, layer, *, tm, ys=None, w_out=None, final_w=None):
    t, d = x2.shape
    f = wg.shape[-1]
    proj = ys is not None
    final = final_w is not None
    tile = lambda width: pl.BlockSpec((tm, width), lambda i: (i, 0))
    row = pl.BlockSpec((1, d), lambda i: (0, 0))
    resident = lambda r, c: pl.BlockSpec((None, r, c), lambda i: (layer, 0, 0), pipeline_mode=pl.Buffered(1))
    ins = [x2] + (list(ys) + [w_out] if proj else []) + [nw, wg, wu, wd] + ([final_w] if final else [])
    in_specs = [tile(d)] + ([tile(W_GROUP)] * 4 + [resident(d, d)] if proj else []) \
        + [row, resident(d, f), resident(d, f), resident(f, d)] + ([row] if final else [])
    return pl.pallas_call(
        functools.partial(_ffn_body, splits=FFN_SPLITS, proj=proj, final=final),
        grid=(t // tm,),
        in_specs=in_specs,
        out_specs=tile(d),
        out_shape=jax.ShapeDtypeStruct((t, d), f32),
        compiler_params=pltpu.CompilerParams(
            dimension_semantics=("parallel",), vmem_limit_bytes=VMEM_LIMIT),
        name="ffn",
    )(*ins)


_SEG_W = (768, 512, 768, 1024, SM_W)
_SEG_O = tuple(int(o) for o in np.cumsum((0,) + _SEG_W))


def _inproj_body(x_ref, nw_ref, w_ref, o_gla, o_lru, o_rw, o_ssd, o_sm):
    x = x_ref[...]
    ms = jnp.mean(x * x, axis=-1, keepdims=True)
    xn = (x * lax.rsqrt(ms + NORM_EPS) * nw_ref[...]).astype(bf16)
    for o_ref, lo, hi in zip((o_gla, o_lru, o_rw, o_ssd, o_sm), _SEG_O[:-1], _SEG_O[1:]):
        o_ref[...] = jnp.dot(xn, w_ref[:, lo:hi], preferred_element_type=f32)


def _inproj(x2, nw, w_in_p, layer, *, tm):
    t, d = x2.shape
    n = w_in_p.shape[-1]
    return pl.pallas_call(
        _inproj_body,
        grid=(t // tm,),
        in_specs=[
            pl.BlockSpec((tm, d), lambda i: (i, 0)),
            pl.BlockSpec((1, d), lambda i: (0, 0)),
            pl.BlockSpec((None, d, n), lambda i: (layer, 0, 0)),
        ],
        out_specs=[pl.BlockSpec((tm, w), lambda i: (i, 0)) for w in _SEG_W],
        out_shape=[jax.ShapeDtypeStruct((t, w), f32) for w in _SEG_W],
        compiler_params=pltpu.CompilerParams(
            dimension_semantics=("parallel",), vmem_limit_bytes=VMEM_LIMIT),
        name="inproj",
    )(x2, nw, w_in_p)


def _gla_body(p_ref, sm_ref, wup_ref, bias_ref, nw_ref, ones_ref, o_ref, st_ref, *, nb):
    c = pl.program_id(0)
    L = GLA_CHUNK
    tc = GLA_TILE
    nch = tc // L

    @pl.when(c == 0)
    def _():
        st_ref[...] = jnp.zeros_like(st_ref)

    row = lax.broadcasted_iota(jnp.int32, (L, 4 * L), 0)
    col = lax.broadcasted_iota(jnp.int32, (L, 4 * L), 1)
    causal = (col % L) <= row
    srow = lax.broadcasted_iota(jnp.int32, (W_GROUP, GLA_DK), 0) // GLA_HV
    scol = lax.broadcasted_iota(jnp.int32, (W_GROUP, GLA_DK), 1) // GLA_HK
    blockdiag = srow == scol
    k_masks = _head_masks(L, GLA_DK)
    v_masks = _head_masks(L, W_GROUP)

    pre = []
    for b in range(nb):
        x = p_ref[b]
        z = _dot(sm_ref[b], wup_ref[...]) + bias_ref[...]
        bc = _cumsum_rows(jax.nn.log_sigmoid(z) / GLA_GATE_NORM, L)
        q_dec = (x[:, 0:128] * (GLA_HK ** -0.5) * jnp.exp(bc)).astype(bf16)
        pre.append((x, bc, q_dec, x[:, 128:256] * jnp.exp(-bc)))
    chains = [(j, b) for j in range(nch) for b in range(nb)]
    intra, kv, dec = {}, {}, {}
    for j, b in chains:
        x, bc, q_dec, k_dec = pre[b]
        rows = slice(j * L, (j + 1) * L)
        s_cat = _dot_nt(q_dec[rows], _stack_heads(k_dec[rows], k_masks))
        s_cat = jnp.where(causal, s_cat, 0.0)
        v = x[rows, 256:512]
        intra[j, b] = _dot(s_cat, _stack_heads(v, v_masks))
        b_last = bc[(j + 1) * L - 1:(j + 1) * L, :]
        k_end = x[rows, 128:256] * jnp.exp(b_last - bc[rows])
        kv[j, b] = jnp.where(blockdiag, _dot_tn(v, k_end), 0.0)
        dec[j, b] = jnp.exp(b_last)
    st = [st_ref[b] for b in range(nb)]
    outs = [[None] * nch for _ in range(nb)]
    for j, b in chains:
        q_dec = pre[b][2]
        outs[b][j] = intra[j, b] + _dot_nt(q_dec[j * L:(j + 1) * L], st[b])
        st[b] = st[b] * dec[j, b] + kv[j, b]
    for b in range(nb):
        st_ref[b] = st[b]
        o = jnp.concatenate(outs[b], axis=0)
        ms = _seg_dot(o * o, ones_ref[...]) * (1.0 / GLA_HV)
        o = o * lax.rsqrt(ms + GROUP_NORM_EPS) * nw_ref[...]
        o_ref[b] = o * jax.nn.silu(pre[b][0][:, 512:768])


def _gla(p_gla, p_sm, wup_pad, bias, nw, ones64):
    nb, s, _ = p_gla.shape
    tc = GLA_TILE
    return pl.pallas_call(
        functools.partial(_gla_body, nb=nb),
        grid=(s // tc,),
        in_specs=[
            pl.BlockSpec((nb, tc, 768), lambda c: (0, c, 0)),
            pl.BlockSpec((nb, tc, SM_W), lambda c: (0, c, 0)),
            pl.BlockSpec((SM_W, GLA_DK), lambda c: (0, 0)),
            pl.BlockSpec((1, GLA_DK), lambda c: (0, 0)),
            pl.BlockSpec((1, W_GROUP), lambda c: (0, 0)),
            pl.BlockSpec((W_GROUP, W_GROUP), lambda c: (0, 0)),
        ],
        out_specs=pl.BlockSpec((nb, tc, W_GROUP), lambda c: (0, c, 0)),
        out_shape=jax.ShapeDtypeStruct((nb, s, W_GROUP), f32),
        scratch_shapes=[pltpu.VMEM((nb, W_GROUP, GLA_DK), f32)],
        compiler_params=pltpu.CompilerParams(dimension_semantics=("arbitrary",)),
        name="gla",
    )(p_gla, p_sm, wup_pad, bias, nw, ones64)


def _lru_body(p_ref, cw_ref, cb_ref, wax_ref, bax_ref, lam_ref, o_ref, ext_ref, h_ref, *, nb):
    c = pl.program_id(0)
    tc = LRU_TILE
    w = W_GROUP

    @pl.when(c == 0)
    def _():
        for b in range(nb):
            ext_ref[b, 0:8, :] = jnp.zeros((8, w), f32)
        h_ref[...] = jnp.zeros_like(h_ref)

    row = lax.broadcasted_iota(jnp.int32, (tc, w), 0)
    for b in range(nb):
        x = p_ref[b]
        xb = x[:, 0:w]
        gate = x[:, w:2 * w]
        ext_ref[b, 8:, :] = xb
        xc = cb_ref[...] + cw_ref[CONV_W - 1:CONV_W, :] * xb
        for j in range(1, CONV_W):
            xc = xc + cw_ref[CONV_W - 1 - j:CONV_W - j, :] * ext_ref[b, pl.ds(8 - j, tc), :]
        ext_ref[b, 0:8, :] = xb[tc - 8:tc, :]
        ri = jax.nn.sigmoid(_dot(xc, wax_ref[...]) + bax_ref[...])
        r = ri[:, 0:w]
        i = ri[:, w:2 * w]
        log_a = -LRU_C * r * jax.nn.softplus(-lam_ref[...])
        a = jnp.exp(log_a)
        u = jnp.sqrt(1.0 - a * a) * (i * xc)
        s = 1
        while s < tc:
            keep = row >= s
            u = u + a * jnp.where(keep, pltpu.roll(u, s, 0), 0.0)
            a = a * jnp.where(keep, pltpu.roll(a, s, 0), 1.0)
            s *= 2
        h = u + a * h_ref[b]
        h_ref[b] = h[tc - 1:tc, :]
        o_ref[b] = h * jax.nn.gelu(gate)


def _lru(p_lru, cw, cb, wax, bax, sp):
    nb, s, _ = p_lru.shape
    tc = LRU_TILE
    w = W_GROUP
    return pl.pallas_call(
        functools.partial(_lru_body, nb=nb),
        grid=(s // tc,),
        in_specs=[
            pl.BlockSpec((nb, tc, 2 * w), lambda c: (0, c, 0)),
            pl.BlockSpec((CONV_W, w), lambda c: (0, 0)),
            pl.BlockSpec((1, w), lambda c: (0, 0)),
            pl.BlockSpec((w, 2 * w), lambda c: (0, 0)),
            pl.BlockSpec((1, 2 * w), lambda c: (0, 0)),
            pl.BlockSpec((1, w), lambda c: (0, 0)),
        ],
        out_specs=pl.BlockSpec((nb, tc, w), lambda c: (0, c, 0)),
        out_shape=jax.ShapeDtypeStruct((nb, s, w), f32),
        scratch_shapes=[pltpu.VMEM((nb, tc + 8, w), f32), pltpu.VMEM((nb, 1, w), f32)],
        compiler_params=pltpu.CompilerParams(dimension_semantics=("arbitrary",)),
        name="rglru",
    )(p_lru, cw, cb, wax, bax, sp)


def _rw_body(*refs, nb, mix):
    if mix:
        (p_ref, sm_ref, vf_ref, mu_ref, mus_ref, w2_ref, a2_ref, g2_ref, vec_ref, v1_ref, v2_ref,
         ones_ref, lvl_ref, o_ref, cx_ref, cs_ref, h_ref) = refs
    else:
        (p_ref, sm_ref, mu_ref, mus_ref, w2_ref, a2_ref, g2_ref, vec_ref,
         ones_ref, lvl_ref, o_ref, vo_ref, cx_ref, cs_ref, h_ref) = refs
    c = pl.program_id(0)
    L = RW_CHUNK
    tc = RW_TILE
    nch = tc // L
    w = W_GROUP
    n4 = 4 * L

    @pl.when(c == 0)
    def _():
        cx_ref[...] = jnp.zeros_like(cx_ref)
        cs_ref[...] = jnp.zeros_like(cs_ref)
        h_ref[...] = jnp.zeros_like(h_ref)

    row3 = lax.broadcasted_iota(jnp.int32, (tc, 3 * w), 0)
    rows_sm = lax.broadcasted_iota(jnp.int32, (tc, SM_W), 0)
    rr = lax.broadcasted_iota(jnp.int32, (n4, n4), 0)
    cc = lax.broadcasted_iota(jnp.int32, (n4, n4), 1)
    strict = (rr % L) > (cc % L)
    incl = (rr % L) >= (cc % L)
    eye = (rr == cc).astype(f32).astype(bf16)
    masks = _head_masks(L, w)
    w0, a0, k_k, k_a, r_k, gn_w, gn_b, v0 = (vec_ref[i:i + 1, :] for i in range(8))
    ones_bf = ones_ref[...]

    pre = []
    for b in range(nb):
        x = p_ref[b]
        sm = sm_ref[b]
        x_prev = jnp.where(row3 == 0, cx_ref[b], pltpu.roll(x, 1, 0))
        sm_prev = jnp.where(rows_sm == 0, cs_ref[b], pltpu.roll(sm, 1, 0))
        cx_ref[b] = x[tc - 1:tc, :]
        cs_ref[b] = sm[tc - 1:tc, :]
        x = x + (x_prev - x) * mu_ref[...]
        sm = sm + (sm_prev - sm) * mus_ref[...]
        r = x[:, 0:w]
        k = x[:, w:2 * w]
        v = x[:, 2 * w:3 * w]
        log_w = -RW_DECAY_SCALE * jax.nn.sigmoid(w0 + _dot(jnp.tanh(sm), w2_ref[...]))
        a = jax.nn.sigmoid(a0 + _dot(sm, a2_ref[...]))
        g = _dot(jax.nn.sigmoid(sm), g2_ref[...])
        if mix:
            lam = jax.nn.sigmoid(v0 + _dot(_dot(v, v1_ref[...]), v2_ref[...]))
            v = v + (vf_ref[b] - v) * lam
        else:
            vo_ref[b] = v
        kk = k * k_k
        n2 = _seg_dot(kk * kk, ones_bf)
        kk = kk / jnp.maximum(jnp.sqrt(n2), 1e-12)
        k = k * (1.0 + (a - 1.0) * k_a)
        cum = _cumsum_rows(log_w, L)
        g_in = jnp.exp(cum)
        g_inv = jnp.exp(-cum)
        pre.append(dict(r=r, k=k, v=v, g=g, g_in=g_in,
                        al=-kk * jnp.exp(cum - log_w),
                        be=kk * a * g_inv, kt=k * g_inv, rt=r * g_in))

    hs = [h_ref[b] for b in range(nb)]
    ys = [[None] * nch for _ in range(nb)]

    def scores(j, b):
        p = pre[b]
        rows = slice(j * L, (j + 1) * L)
        g_last = p["g_in"][(j + 1) * L - 1:(j + 1) * L, :]
        d = dict(j=j, b=b)
        d["g_col"] = jnp.broadcast_to(jnp.broadcast_to(g_last, (8, w)).T[:, 0:1], (w, w))
        d["al"] = _stack_heads(p["al"][rows], masks)
        d["rt"] = _stack_heads(p["rt"][rows], masks)
        d["vst"] = _stack_heads(p["v"][rows], masks)
        be = _stack_heads(p["be"][rows], masks)
        kt = _stack_heads(p["kt"][rows], masks)
        d["be_end"] = _stack_heads(p["be"][rows] * g_last, masks)
        kt_end = _stack_heads(p["kt"][rows] * g_last, masks)
        aa = _dot_nt(jnp.concatenate([d["al"], d["rt"]], axis=0), jnp.concatenate([be, kt], axis=0))
        d["a_ab"] = jnp.where(strict, aa[0:n4, 0:n4], 0.0).astype(bf16)
        d["a_ak"] = jnp.where(strict, aa[0:n4, n4:2 * n4], 0.0).astype(bf16)
        d["a_rb"] = jnp.where(incl, aa[n4:2 * n4, 0:n4], 0.0).astype(bf16)
        d["a_rk"] = jnp.where(incl, aa[n4:2 * n4, n4:2 * n4], 0.0).astype(bf16)
        d["kv"] = _dot_tn(kt_end, d["vst"])
        return d

    def invert(ch):
        for d in ch:
            d["t"] = eye + d["a_ab"] * lvl_ref[0]
        for lv in range(1, 6):
            x1 = [_dot(d["t"], d["a_ab"] * lvl_ref[lv]) for d in ch]
            for d, x in zip(ch, x1):
                d["t"] = d["t"] + _dot(x, d["t"]).astype(bf16)
        for d in ch:
            ak = _dot(jnp.concatenate([d["a_ak"], d["a_rk"]], axis=0), d["vst"])
            pq = _dot(d["t"], jnp.concatenate([d["al"], ak[0:n4].astype(bf16)], axis=1))
            d["p_st"] = pq[:, 0:w].astype(bf16)
            d["q_st"] = pq[:, w:2 * w]
            d["y_k"] = ak[n4:2 * n4]

    def link(d):
        b = d["b"]
        h_bf = hs[b].astype(bf16)
        u = (_dot(d["p_st"], h_bf) + d["q_st"]).astype(bf16)
        hs[b] = hs[b] * d["g_col"] + _dot_tn(d["be_end"], u) + d["kv"]
        y_st = _dot(d["rt"], h_bf) + _dot(d["a_rb"], u) + d["y_k"]
        ys[b][d["j"]] = y_st[0:L] + y_st[L:2 * L] + y_st[2 * L:3 * L] + y_st[3 * L:4 * L]

    ch = [scores(j, b) for j in range(nch) for b in range(nb)]
    invert(ch)
    for d in ch:
        link(d)

    for b in range(nb):
        h_ref[b] = hs[b]
        p = pre[b]
        y = jnp.concatenate(ys[b], axis=0)
        mean = _seg_dot(y, ones_bf) * (1.0 / RW_HS)
        dlt = y - mean
        var = _seg_dot(dlt * dlt, ones_bf) * (1.0 / RW_HS)
        yn = dlt * lax.rsqrt(var + RW_GN_EPS) * gn_w + gn_b
        bonus = _seg_dot(p["r"] * p["k"] * r_k, ones_bf) * p["v"]
        o_ref[b] = (yn + bonus) * p["g"]


def _rwkv(p_rw, p_sm, v_first, mu, mus, w2p, a2p, g2p, vecs, v1p, v2p, ones64, lvl):
    nb, s, _ = p_rw.shape
    tc = RW_TILE
    w = W_GROUP
    mix = v_first is not None
    blk = lambda width: pl.BlockSpec((nb, tc, width), lambda c: (0, c, 0))
    full = lambda a: pl.BlockSpec(a.shape, lambda c: (0,) * a.ndim)
    ins = [p_rw, p_sm] + ([v_first] if mix else []) + [mu, mus, w2p, a2p, g2p, vecs] \
        + ([v1p, v2p] if mix else []) + [ones64, lvl]
    in_specs = [blk(3 * w), blk(SM_W)] + ([blk(w)] if mix else []) + [full(a) for a in ins[(3 if mix else 2):]]
    out_shape = jax.ShapeDtypeStruct((nb, s, w), f32)
    return pl.pallas_call(
        functools.partial(_rw_body, nb=nb, mix=mix),
        grid=(s // tc,),
        in_specs=in_specs,
        out_specs=blk(w) if mix else [blk(w), blk(w)],
        out_shape=out_shape if mix else [out_shape, out_shape],
        scratch_shapes=[pltpu.VMEM((nb, 1, 3 * w), f32), pltpu.VMEM((nb, 1, SM_W), f32),
                        pltpu.VMEM((nb, w, w), f32)],
        compiler_params=pltpu.CompilerParams(
            dimension_semantics=("arbitrary",), vmem_limit_bytes=VMEM_LIMIT),
        name="rwkv7",
    )(*ins)


def _ssd_body(p_ref, sm_ref, cw_ref, cb_ref, dtb_ref, alog_ref, dsk_ref, nw_ref, o_ref, ext_ref, h_ref, *, nb):
    c = pl.program_id(0)
    L = SSD_CHUNK
    w = W_GROUP
    cdim = 3 * w

    @pl.when(c == 0)
    def _():
        for b in range(nb):
            ext_ref[b, 0:8, :] = jnp.zeros((8, cdim), f32)
        h_ref[...] = jnp.zeros_like(h_ref)

    ll = lax.broadcasted_iota(jnp.int32, (L, L), 0)
    ss = lax.broadcasted_iota(jnp.int32, (L, L), 1)
    causal = ll >= ss
    lane_head = lax.broadcasted_iota(jnp.int32, (L, w), 1) // (w // SSD_HEADS)

    for b in range(nb):
        x = p_ref[b]
        z = x[:, 0:w]
        xbc = x[:, w:4 * w]
        ext_ref[b, 8:, :] = xbc
        conv = cb_ref[...] + cw_ref[CONV_W - 1:CONV_W, :] * xbc
        for j in range(1, CONV_W):
            conv = conv + cw_ref[CONV_W - 1 - j:CONV_W - j, :] * ext_ref[b, pl.ds(8 - j, L), :]
        ext_ref[b, 0:8, :] = xbc[L - 8:L, :]
        xbc = jax.nn.silu(conv)
        xs = xbc[:, 0:w]
        bm = xbc[:, w:2 * w]
        cm = xbc[:, 2 * w:3 * w]

        dt = jax.nn.softplus(sm_ref[b] + dtb_ref[...])
        d_a = dt * (-jnp.exp(alog_ref[0:1, :]) * alog_ref[1:2, :])
        cs = _cumsum_rows(d_a)
        cs_t = cs.T
        dt_full = _expand_heads(dt, SM_DT, w)
        cs_full = _expand_heads(cs, SM_DT, w)
        cs_last = cs_full[L - 1:L, :]
        xdt = xs * dt_full

        h_prev = h_ref[b]
        bm_t = bm.T
        y = jnp.zeros((L, w), f32)
        y_off = []
        st = []
        xw = xdt * jnp.exp(cs_last - cs_full)
        for grp in range(2):
            lo, hi = grp * SSD_DSTATE, (grp + 1) * SSD_DSTATE
            cb_g = _dot_nt(cm[:, lo:hi], bm[:, lo:hi])
            for r in range(2):
                hd = grp * 2 + r
                seg = cs[:, SM_DT + hd:SM_DT + hd + 1] - cs_t[SM_DT + hd:SM_DT + hd + 1, :]
                m = cb_g * jnp.where(causal, jnp.exp(seg), 0.0)
                y = y + _dot(m, jnp.where(lane_head == hd, xdt, 0.0))
            y_off.append(_dot(cm[:, lo:hi], h_prev[:, lo:hi]))
            st.append(_dot(bm_t[lo:hi, :], xw[:, lo:hi]))
        y = y + jnp.concatenate(y_off, axis=1) * jnp.exp(cs_full)
        h_ref[b] = h_prev * jnp.exp(cs_last) + jnp.concatenate(st, axis=1)
        y = y + dsk_ref[...] * xs
        y = y * jax.nn.silu(z)
        outs = []
        for grp in range(2):
            yg = y[:, grp * 128:(grp + 1) * 128]
            ms = jnp.mean(yg * yg, axis=-1, keepdims=True)
            outs.append(yg * lax.rsqrt(ms + GROUP_NORM_EPS))
        o_ref[b] = jnp.concatenate(outs, axis=1) * nw_ref[...]


def _ssd(p_ssd, p_sm, cw, cb, dtb_row, a_row, dsk, nw):
    nb, s, _ = p_ssd.shape
    L = SSD_CHUNK
    w = W_GROUP
    full = lambda a: pl.BlockSpec(a.shape, lambda c: (0,) * a.ndim)
    return pl.pallas_call(
        functools.partial(_ssd_body, nb=nb),
        grid=(s // L,),
        in_specs=[pl.BlockSpec((nb, L, 4 * w), lambda c: (0, c, 0)),
                  pl.BlockSpec((nb, L, SM_W), lambda c: (0, c, 0))]
        + [full(a) for a in (cw, cb, dtb_row, a_row, dsk, nw)],
        out_specs=pl.BlockSpec((nb, L, w), lambda c: (0, c, 0)),
        out_shape=jax.ShapeDtypeStruct((nb, s, w), f32),
        scratch_shapes=[pltpu.VMEM((nb, L + 8, 3 * w), f32), pltpu.VMEM((nb, SSD_DSTATE, w), f32)],
        compiler_params=pltpu.CompilerParams(dimension_semantics=("arbitrary",)),
        name="ssd",
    )(p_ssd, p_sm, cw, cb, dtb_row, a_row, dsk, nw)


def _level_masks():
    n = 4 * RW_CHUNK
    t = np.arange(n)[:, None]
    u = np.arange(n)[None, :]
    out = []
    s = 1
    while s < RW_CHUNK:
        out.append(((t // (2 * s)) == (u // (2 * s))) & ((t % (2 * s)) >= s) & ((u % (2 * s)) < s))
        s *= 2
    return jnp.asarray(np.stack(out).astype(np.float32), dtype=bf16)


def _ones_blocks(seg):
    i = np.arange(W_GROUP)
    return jnp.asarray((i[:, None] // seg == i[None, :] // seg).astype(np.float32), dtype=bf16)


def _pad_rows(m, row0, rows):
    return jnp.zeros((rows, m.shape[1]), m.dtype).at[row0:row0 + m.shape[0], :].set(m)


def _lane_row(v, lane0, width):
    return jnp.zeros((1, width), f32).at[0, lane0:lane0 + v.shape[0]].set(v)


def _block_diag(w):
    n, k, j = w.shape
    out = jnp.zeros((n * k, n * j), w.dtype)
    for i in range(n):
        out = out.at[i * k:(i + 1) * k, i * j:(i + 1) * j].set(w[i])
    return out


def kernel(x, ffn1_norm, ffn1_w_gate, ffn1_w_up, ffn1_w_down, mix_norm, w_in, w_out, gla_alpha_up, gla_alpha_bias, gla_norm, lru_conv_w, lru_conv_b, lru_w_a, lru_b_a, lru_w_x, lru_b_x, lru_lambda, rw_mu, rw_w0, rw_w2, rw_a0, rw_a2, rw_g2, rw_v0, rw_v1, rw_v2, rw_k_k, rw_k_a, rw_r_k, rw_gn_w, rw_gn_b, ssd_conv_w, ssd_conv_b, ssd_dt_bias, ssd_a_log, ssd_d, ssd_norm, ffn2_norm, ffn2_w_gate, ffn2_w_up, ffn2_w_down, final_norm):
    nb, s, d = x.shape
    t = nb * s
    depth = w_in.shape[0]
    tm = min(512, t)
    tm_in = min(256, t)

    wg1, wu1, wd1 = (a.astype(bf16) for a in (ffn1_w_gate, ffn1_w_up, ffn1_w_down))
    wg2, wu2, wd2 = (a.astype(bf16) for a in (ffn2_w_gate, ffn2_w_up, ffn2_w_down))
    w_out_b = w_out.astype(bf16)
    pad = jnp.zeros((depth, d, SM_W - 84), w_in.dtype)
    w_in_p = jnp.concatenate([
        w_in[:, :, _GLA0:_GLA0 + 768], w_in[:, :, _LRU0:_LRU0 + 512], w_in[:, :, _RW0:_RW0 + 768],
        w_in[:, :, _SSD0:_SSD0 + 1024],
        w_in[:, :, _GLA0 + 768:_GLA0 + 784], w_in[:, :, _RW0 + 768:_RW0 + 832],
        w_in[:, :, _SSD0 + 1024:_SSD0 + 1028], pad], axis=-1).astype(bf16)
    ones64 = _ones_blocks(64)
    lvl = _level_masks()

    x2 = x.reshape(t, d)
    v_first = None
    for l in range(depth):
        x2 = _ffn(x2, ffn1_norm[l][None, :], wg1, wu1, wd1, l, tm=tm)
        p_gla, p_lru, p_rw, p_ssd, p_sm = _inproj(x2, mix_norm[l][None, :], w_in_p, l, tm=tm_in)
        p_gla, p_lru, p_rw, p_ssd, p_sm = (a.reshape(nb, s, a.shape[-1]) for a in (p_gla, p_lru, p_rw, p_ssd, p_sm))

        y_gla = _gla(p_gla, p_sm, _pad_rows(gla_alpha_up[l], SM_GLA, SM_W), gla_alpha_bias[l][None, :],
                     jnp.tile(gla_norm[l], GLA_HEADS)[None, :], ones64)

        wax = jnp.concatenate([_block_diag(lru_w_a[l]), _block_diag(lru_w_x[l])], axis=1).astype(bf16)
        bax = jnp.concatenate([lru_b_a[l], lru_b_x[l]])[None, :]
        y_lru = _lru(p_lru, lru_conv_w[l], lru_conv_b[l][None, :], wax, bax,
                     lru_lambda[l][None, :])

        vecs = jnp.stack([rw_w0[l], rw_a0[l], rw_k_k[l], rw_k_a[l], rw_r_k[l].reshape(-1), rw_gn_w[l],
                          rw_gn_b[l], rw_v0[l - 1] if l > 0 else jnp.zeros_like(rw_w0[l])])
        mu = rw_mu[l]
        rw_args = (mu[None, 0:768], _lane_row(mu[768:832], SM_RW, SM_W),
                   _pad_rows(rw_w2[l], SM_RW, SM_W), _pad_rows(rw_a2[l], SM_RW + 16, SM_W),
                   _pad_rows(rw_g2[l], SM_RW + 32, SM_W), vecs)
        if l == 0:
            y_rw, v_first = _rwkv(p_rw, p_sm, None, *rw_args, None, None, ones64, lvl)
        else:
            v1p = jnp.zeros((W_GROUP, 128), f32).at[:, 0:rw_v1.shape[-1]].set(rw_v1[l - 1])
            v2p = _pad_rows(rw_v2[l - 1], 0, 128)
            y_rw = _rwkv(p_rw, p_sm, v_first, *rw_args, v1p, v2p, ones64, lvl)

        y_ssd = _ssd(p_ssd, p_sm, ssd_conv_w[l], ssd_conv_b[l][None, :],
                     _lane_row(ssd_dt_bias[l], SM_DT, SM_W), jnp.concatenate([_lane_row(ssd_a_log[l], SM_DT, SM_W),
                                      _lane_row(jnp.ones((SSD_HEADS,), f32), SM_DT, SM_W)]),
                     jnp.repeat(ssd_d[l], W_GROUP // SSD_HEADS)[None, :], ssd_norm[l][None, :])

        ys = [a.reshape(t, W_GROUP) for a in (y_gla, y_lru, y_rw, y_ssd)]
        x2 = _ffn(x2, ffn2_norm[l][None, :], wg2, wu2, wd2, l, tm=tm, ys=ys, w_out=w_out_b,
                  final_w=final_norm[None, :] if l == depth - 1 else None)
    return x2.reshape(nb, s, d)
```

```python
import functools
import math

import numpy as np
import jax
import jax.numpy as jnp
from jax import lax
from jax.experimental import pallas as pl
from jax.experimental.pallas import tpu as pltpu

f32 = jnp.float32
bf16 = jnp.bfloat16

D_MODEL = 1024
D_FF = 2816
W_GROUP = 256
NORM_EPS = 1e-6
GROUP_NORM_EPS = 1e-5
CONV_W = 4

GLA_HEADS = 4
GLA_DK = 128
GLA_HK = 32
GLA_HV = 64
GLA_GATE_NORM = 16.0
GLA_CHUNK = 64

LRU_C = 8.0

RW_HS = 64
RW_DECAY_SCALE = math.exp(-0.5)
RW_GN_EPS = 64e-5
RW_CHUNK = 64

SSD_HEADS = 4
SSD_DSTATE = 128
SSD_CHUNK = 128

_GLA0, _LRU0, _RW0, _SSD0 = 0, 784, 1296, 2128
_SEGS = dict(gla=(0, 768), lru=(768, 1280), rw=(1280, 2048), ssd=(2048, 3072), sm=(3072, 3200))
SM_GLA = 0
SM_RW = 16
SM_DT = 80
SM_W = 128

MIX_TILE = 256
MXU_TILE = 256
FFN_SPLITS = (MXU_TILE,) * (D_FF // MXU_TILE)
VMEM_LIMIT = 56 * 1024 * 1024


def _dot(a, b):
    return jnp.dot(a.astype(bf16), b.astype(bf16), preferred_element_type=f32)


def _dot_nt(a, b):
    return lax.dot_general(a.astype(bf16), b.astype(bf16), (((1,), (1,)), ((), ())),
                           preferred_element_type=f32)


def _dot_tn(a, b):
    return lax.dot_general(a.astype(bf16), b.astype(bf16), (((0,), (0,)), ((), ())),
                           preferred_element_type=f32)


def _seg_dot(x, ones_bf):
    hi = x.astype(bf16)
    lo = (x - hi.astype(f32)).astype(bf16)
    return (jnp.dot(hi, ones_bf, preferred_element_type=f32)
            + jnp.dot(lo, ones_bf, preferred_element_type=f32))


def _cumsum_rows(x, seg):
    row = lax.broadcasted_iota(jnp.int32, x.shape, 0) % seg
    s = 1
    while s < seg:
        x = x + jnp.where(row >= s, pltpu.roll(x, s, 0), 0.0)
        s *= 2
    return x


def _head_masks(rows, width):
    lane = lax.broadcasted_iota(jnp.int32, (rows, width), 1) // (width // 4)
    return [(lane == h).astype(f32).astype(bf16) for h in range(4)]


def _stack_heads(x, masks):
    xb = x.astype(bf16)
    return jnp.concatenate([xb * m for m in masks], axis=0)


def _expand_heads(x, lane0, width):
    n = x.shape[0]
    lane = lax.broadcasted_iota(jnp.int32, (n, width), 1) // (width // 4)
    out = jnp.zeros((n, width), f32)
    for h in range(4):
        out = jnp.where(lane == h, x[:, lane0 + h:lane0 + h + 1], out)
    return out


def _rms_norm(x, w):
    ms = jnp.mean(x * x, axis=-1, keepdims=True)
    return x * lax.rsqrt(ms + NORM_EPS) * w


def _merge(*stage_lists):
    keyed = []
    for k, stages in enumerate(stage_lists):
        keyed += [((i + 0.5) / len(stages), k, i, st) for i, st in enumerate(stages)]
    return [st for _, _, _, st in sorted(keyed, key=lambda t: t[:3])]


def _ffn_body(*refs, splits, proj, final):
    refs = list(refs)
    x_ref = refs.pop(0)
    if proj:
        y_refs = [refs.pop(0) for _ in range(4)]
        wo_ref = refs.pop(0)
    nw_ref, wg_ref, wu_ref, wd_ref = (refs.pop(0) for _ in range(4))
    if final:
        fw_ref = refs.pop(0)
    o_ref = refs.pop(0)

    x = x_ref[...]
    if proj:
        for i, y_ref in enumerate(y_refs):
            x = x + jnp.dot(y_ref[...].astype(bf16), wo_ref[i * W_GROUP:(i + 1) * W_GROUP, :],
                            preferred_element_type=f32)
    xn = _rms_norm(x, nw_ref[...]).astype(bf16)
    acc = None
    lo = 0
    for width in splits:
        hi = lo + width
        g = jnp.dot(xn, wg_ref[:, lo:hi], preferred_element_type=f32)
        u = jnp.dot(xn, wu_ref[:, lo:hi], preferred_element_type=f32)
        h = (jax.nn.silu(g) * u).astype(bf16)
        part = jnp.dot(h, wd_ref[lo:hi, :], preferred_element_type=f32)
        acc = part if acc is None else acc + part
        lo = hi
    out = x + 0.5 * acc
    if final:
        out = _rms_norm(out, fw_ref[...])
    o_ref[...] = out


def _ffn(x2, nw, wg, wu, wd, layer, *, tm, ys=None, w_out=None, final_w=None):
    t, d = x2.shape
    f = wg.shape[-1]
    proj = ys is not None
    final = final_w is not None
    tile = lambda width: pl.BlockSpec((tm, width), lambda i: (i, 0))
    row = pl.BlockSpec((1, d), lambda i: (0, 0))
    resident = lambda r, c: pl.BlockSpec((None, r, c), lambda i: (layer, 0, 0), pipeline_mode=pl.Buffered(1))
    ins = [x2] + (list(ys) + [w_out] if proj else []) + [nw, wg, wu, wd] + ([final_w] if final else [])
    in_specs = [tile(d)] + ([tile(W_GROUP)] * 4 + [resident(d, d)] if proj else []) \
        + [row, resident(d, f), resident(d, f), resident(f, d)] + ([row] if final else [])
    return pl.pallas_call(
        functools.partial(_ffn_body, splits=FFN_SPLITS, proj=proj, final=final),
        grid=(t // tm,),
        in_specs=in_specs,
        out_specs=tile(d),
        out_shape=jax.ShapeDtypeStruct((t, d), f32),
        compiler_params=pltpu.CompilerParams(
            dimension_semantics=("parallel",), vmem_limit_bytes=VMEM_LIMIT),
        name="ffn",
    )(*ins)


def _gla_stages(R, P, nb):
    L = GLA_CHUNK
    tc = MIX_TILE
    nch = tc // L
    row = lax.broadcasted_iota(jnp.int32, (L, 4 * L), 0)
    col = lax.broadcasted_iota(jnp.int32, (L, 4 * L), 1)
    causal = (col % L) <= row
    srow = lax.broadcasted_iota(jnp.int32, (W_GROUP, GLA_DK), 0) // GLA_HV
    scol = lax.broadcasted_iota(jnp.int32, (W_GROUP, GLA_DK), 1) // GLA_HK
    blockdiag = srow == scol
    k_masks = _head_masks(L, GLA_DK)
    v_masks = _head_masks(L, W_GROUP)
    pre, intra, kv, dec = {}, {}, {}, {}
    st = {}
    outs = [[None] * nch for _ in range(nb)]

    def prep(b):
        x = P[b]["gla"]
        z = _dot(P[b]["sm"], R["g_wup"][...]) + R["g_bias"][...]
        bc = _cumsum_rows(jax.nn.log_sigmoid(z) / GLA_GATE_NORM, L)
        q_dec = (x[:, 0:128] * (GLA_HK ** -0.5) * jnp.exp(bc)).astype(bf16)
        pre[b] = (x, bc, q_dec, x[:, 128:256] * jnp.exp(-bc))
        st[b] = R["g_st"][b]

    def chain(j, b):
        x, bc, q_dec, k_dec = pre[b]
        rows = slice(j * L, (j + 1) * L)
        s_cat = _dot_nt(q_dec[rows], _stack_heads(k_dec[rows], k_masks))
        s_cat = jnp.where(causal, s_cat, 0.0)
        v = x[rows, 256:512]
        intra[j, b] = _dot(s_cat, _stack_heads(v, v_masks))
        b_last = bc[(j + 1) * L - 1:(j + 1) * L, :]
        k_end = x[rows, 128:256] * jnp.exp(b_last - bc[rows])
        kv[j, b] = jnp.where(blockdiag, _dot_tn(v, k_end), 0.0)
        dec[j, b] = jnp.exp(b_last)

    def link(j, b):
        q_dec = pre[b][2]
        outs[b][j] = intra[j, b] + _dot_nt(q_dec[j * L:(j + 1) * L], st[b])
        st[b] = st[b] * dec[j, b] + kv[j, b]

    def finish(b):
        R["g_st"][b] = st[b]
        o = jnp.concatenate(outs[b], axis=0)
        ms = _seg_dot(o * o, R["ones"][...]) * (1.0 / GLA_HV)
        o = o * lax.rsqrt(ms + GROUP_NORM_EPS) * R["g_nw"][...]
        R["o_gla"][b] = o * jax.nn.silu(pre[b][0][:, 512:768])

    P_ = functools.partial
    chains = [(j, b) for j in range(nch) for b in range(nb)]
    return ([P_(prep, b) for b in range(nb)] + [P_(chain, j, b) for j, b in chains]
            + [P_(link, j, b) for j, b in chains] + [P_(finish, b) for b in range(nb)])


def _lru_stages(R, P, nb):
    tc = MIX_TILE
    w = W_GROUP
    row = lax.broadcasted_iota(jnp.int32, (tc, w), 0)
    au = {}

    def gates(b):
        x = P[b]["lru"]
        xb = x[:, 0:w]
        ext = R["l_ext"]
        ext[b, 8:, :] = xb
        xc = R["l_cb"][...] + R["l_cw"][CONV_W - 1:CONV_W, :] * xb
        for j in range(1, CONV_W):
            xc = xc + R["l_cw"][CONV_W - 1 - j:CONV_W - j, :] * ext[b, pl.ds(8 - j, tc), :]
        ext[b, 0:8, :] = xb[tc - 8:tc, :]
        ri = jax.nn.sigmoid(_dot(xc, R["l_wax"][...]) + R["l_bax"][...])
        log_a = -LRU_C * ri[:, 0:w] * jax.nn.softplus(-R["l_lam"][...])
        a = jnp.exp(log_a)
        au[b] = (a, jnp.sqrt(1.0 - a * a) * (ri[:, w:2 * w] * xc))

    def scan(b, shifts):
        a, u = au[b]
        for s in shifts:
            keep = row >= s
            u = u + a * jnp.where(keep, pltpu.roll(u, s, 0), 0.0)
            a = a * jnp.where(keep, pltpu.roll(a, s, 0), 1.0)
        au[b] = (a, u)

    def finish(b):
        a, u = au[b]
        h = u + a * R["l_h"][b]
        R["l_h"][b] = h[tc - 1:tc, :]
        R["o_lru"][b] = h * jax.nn.gelu(P[b]["lru"][:, w:2 * w])

    shifts = [1 << i for i in range(tc.bit_length() - 1)]
    half = len(shifts) // 2
    P_ = functools.partial
    out = []
    for b in range(nb):
        out += [P_(gates, b), P_(scan, b, shifts[:half]), P_(scan, b, shifts[half:]), P_(finish, b)]
    return out


def _rw_stages(R, P, nb, mix):
    L = RW_CHUNK
    tc = MIX_TILE
    nch = tc // L
    w = W_GROUP
    n4 = 4 * L
    row3 = lax.broadcasted_iota(jnp.int32, (tc, 3 * w), 0)
    rows_sm = lax.broadcasted_iota(jnp.int32, (tc, SM_W), 0)
    rr = lax.broadcasted_iota(jnp.int32, (n4, n4), 0)
    cc = lax.broadcasted_iota(jnp.int32, (n4, n4), 1)
    strict = (rr % L) > (cc % L)
    incl = (rr % L) >= (cc % L)
    eye = (rr == cc).astype(f32).astype(bf16)
    masks = _head_masks(L, w)
    w0, a0, k_k, k_a, r_k, gn_w, gn_b, v0 = (R["r_vec"][i:i + 1, :] for i in range(8))
    lvl = R["lvl"]
    pre, hs = {}, {}
    ch = {}
    ys = [[None] * nch for _ in range(nb)]
    keys = [(j, b) for j in range(nch) for b in range(nb)]

    def prep(b):
        x = P[b]["rw"]
        sm = P[b]["sm"]
        x_prev = jnp.where(row3 == 0, R["r_cx"][b], pltpu.roll(x, 1, 0))
        sm_prev = jnp.where(rows_sm == 0, R["r_cs"][b], pltpu.roll(sm, 1, 0))
        R["r_cx"][b] = x[tc - 1:tc, :]
        R["r_cs"][b] = sm[tc - 1:tc, :]
        x = x + (x_prev - x) * R["r_mu"][...]
        sm = sm + (sm_prev - sm) * R["r_mus"][...]
        r = x[:, 0:w]
        k = x[:, w:2 * w]
        v = x[:, 2 * w:3 * w]
        log_w = -RW_DECAY_SCALE * jax.nn.sigmoid(w0 + _dot(jnp.tanh(sm), R["r_w2"][...]))
        a = jax.nn.sigmoid(a0 + _dot(sm, R["r_a2"][...]))
        g = _dot(jax.nn.sigmoid(sm), R["r_g2"][...])
        if mix:
            lam = jax.nn.sigmoid(v0 + _dot(_dot(v, R["r_v1"][...]), R["r_v2"][...]))
            v = v + (R["vf"][b] - v) * lam
        else:
            R["o_v"][b] = v
        kk = k * k_k
        n2 = _seg_dot(kk * kk, R["ones"][...])
        kk = kk / jnp.maximum(jnp.sqrt(n2), 1e-12)
        k = k * (1.0 + (a - 1.0) * k_a)
        cum = _cumsum_rows(log_w, L)
        g_in = jnp.exp(cum)
        g_inv = jnp.exp(-cum)
        pre[b] = dict(r=r, k=k, v=v, g=g, g_in=g_in,
                      al=-kk * jnp.exp(cum - log_w),
                      be=kk * a * g_inv, kt=k * g_inv, rt=r * g_in)
        hs[b] = R["r_h"][b]

    def scores(j, b):
        p = pre[b]
        rows = slice(j * L, (j + 1) * L)
        g_last = p["g_in"][(j + 1) * L - 1:(j + 1) * L, :]
        d = dict()
        d["g_col"] = jnp.broadcast_to(jnp.broadcast_to(g_last, (8, w)).T[:, 0:1], (w, w))
        d["al"] = _stack_heads(p["al"][rows], masks)
        d["rt"] = _stack_heads(p["rt"][rows], masks)
        d["vst"] = _stack_heads(p["v"][rows], masks)
        be = _stack_heads(p["be"][rows], masks)
        kt = _stack_heads(p["kt"][rows], masks)
        d["be_end"] = _stack_heads(p["be"][rows] * g_last, masks)
        kt_end = _stack_heads(p["kt"][rows] * g_last, masks)
        aa = _dot_nt(jnp.concatenate([d["al"], d["rt"]], axis=0), jnp.concatenate([be, kt], axis=0))
        d["a_ab"] = jnp.where(strict, aa[0:n4, 0:n4], 0.0).astype(bf16)
        d["a_ak"] = jnp.where(strict, aa[0:n4, n4:2 * n4], 0.0).astype(bf16)
        d["a_rb"] = jnp.where(incl, aa[n4:2 * n4, 0:n4], 0.0).astype(bf16)
        d["a_rk"] = jnp.where(incl, aa[n4:2 * n4, n4:2 * n4], 0.0).astype(bf16)
        d["kv"] = _dot_tn(kt_end, d["vst"])
        ch[j, b] = d

    def level(lv):
        if lv == 0:
            for key in keys:
                ch[key]["t"] = eye + ch[key]["a_ab"] * lvl[0]
            return
        x1 = [_dot(ch[key]["t"], ch[key]["a_ab"] * lvl[lv]) for key in keys]
        for key, x in zip(keys, x1):
            ch[key]["t"] = ch[key]["t"] + _dot(x, ch[key]["t"]).astype(bf16)

    def solve(j, b):
        d = ch[j, b]
        ak = _dot(jnp.concatenate([d["a_ak"], d["a_rk"]], axis=0), d["vst"])
        pq = _dot(d["t"], jnp.concatenate([d["al"], ak[0:n4].astype(bf16)], axis=1))
        d["p_st"] = pq[:, 0:w].astype(bf16)
        d["q_st"] = pq[:, w:2 * w]
        d["y_k"] = ak[n4:2 * n4]

    def link(j, b):
        d = ch[j, b]
        h_bf = hs[b].astype(bf16)
        u = (_dot(d["p_st"], h_bf) + d["q_st"]).astype(bf16)
        hs[b] = hs[b] * d["g_col"] + _dot_tn(d["be_end"], u) + d["kv"]
        y_st = _dot(d["rt"], h_bf) + _dot(d["a_rb"], u) + d["y_k"]
        ys[b][j] = y_st[0:L] + y_st[L:2 * L] + y_st[2 * L:3 * L] + y_st[3 * L:4 * L]

    def finish(b):
        R["r_h"][b] = hs[b]
        p = pre[b]
        y = jnp.concatenate(ys[b], axis=0)
        mean = _seg_dot(y, R["ones"][...]) * (1.0 / RW_HS)
        dlt = y - mean
        var = _seg_dot(dlt * dlt, R["ones"][...]) * (1.0 / RW_HS)
        yn = dlt * lax.rsqrt(var + RW_GN_EPS) * gn_w + gn_b
        bonus = _seg_dot(p["r"] * p["k"] * r_k, R["ones"][...]) * p["v"]
        R["o_rw"][b] = (yn + bonus) * p["g"]

    P_ = functools.partial
    n_lvl = RW_CHUNK.bit_length() - 1
    return ([P_(prep, b) for b in range(nb)],
            [P_(scores, j, b) for j, b in keys] + [P_(level, lv) for lv in range(n_lvl)]
            + [P_(solve, j, b) for j, b in keys],
            [P_(link, j, b) for j, b in keys] + [P_(finish, b) for b in range(nb)])


def _ssd_stages(R, P, nb):
    L = SSD_CHUNK
    tc = MIX_TILE
    nch = tc // L
    w = W_GROUP
    cdim = 3 * w
    ll = lax.broadcasted_iota(jnp.int32, (L, L), 0)
    ss = lax.broadcasted_iota(jnp.int32, (L, L), 1)
    causal = ll >= ss
    lane_head = lax.broadcasted_iota(jnp.int32, (L, w), 1) // (w // SSD_HEADS)
    pre, hst = {}, {}
    ys = [[None] * nch for _ in range(nb)]

    def prep(b):
        x = P[b]["ssd"]
        xbc = x[:, w:4 * w]
        ext = R["s_ext"]
        ext[b, 8:, :] = xbc
        conv = R["s_cb"][...] + R["s_cw"][CONV_W - 1:CONV_W, :] * xbc
        for j in range(1, CONV_W):
            conv = conv + R["s_cw"][CONV_W - 1 - j:CONV_W - j, :] * ext[b, pl.ds(8 - j, tc), :]
        ext[b, 0:8, :] = xbc[tc - 8:tc, :]
        xbc = jax.nn.silu(conv)
        dt = jax.nn.softplus(P[b]["sm"] + R["s_dtb"][...])
        d_a = dt * (-jnp.exp(R["s_alog"][0:1, :]) * R["s_alog"][1:2, :])
        cs = _cumsum_rows(d_a, L)
        pre[b] = dict(xbc=xbc, cs=cs, dt_full=_expand_heads(dt, SM_DT, w), cs_full=_expand_heads(cs, SM_DT, w))
        hst[b] = R["s_h"][b]

    def chunk(j, b):
        p = pre[b]
        rows = slice(j * L, (j + 1) * L)
        xs = p["xbc"][rows, 0:w]
        bm = p["xbc"][rows, w:2 * w]
        cm = p["xbc"][rows, 2 * w:3 * w]
        cs = p["cs"][rows]
        cs_t = cs.T
        cs_full = p["cs_full"][rows]
        cs_last = cs_full[L - 1:L, :]
        xdt = xs * p["dt_full"][rows]
        h_prev = hst[b]
        bm_t = bm.T
        y = jnp.zeros((L, w), f32)
        y_off = []
        st = []
        xw = xdt * jnp.exp(cs_last - cs_full)
        for grp in range(2):
            lo, hi = grp * SSD_DSTATE, (grp + 1) * SSD_DSTATE
            cb_g = _dot_nt(cm[:, lo:hi], bm[:, lo:hi])
            for r in range(2):
                hd = grp * 2 + r
                seg = cs[:, SM_DT + hd:SM_DT + hd + 1] - cs_t[SM_DT + hd:SM_DT + hd + 1, :]
                m = cb_g * jnp.where(causal, jnp.exp(seg), 0.0)
                y = y + _dot(m, jnp.where(lane_head == hd, xdt, 0.0))
            y_off.append(_dot(cm[:, lo:hi], h_prev[:, lo:hi]))
            st.append(_dot(bm_t[lo:hi, :], xw[:, lo:hi]))
        y = y + jnp.concatenate(y_off, axis=1) * jnp.exp(cs_full)
        hst[b] = h_prev * jnp.exp(cs_last) + jnp.concatenate(st, axis=1)
        ys[b][j] = y + R["s_dsk"][...] * xs

    def finish(b):
        R["s_h"][b] = hst[b]
        y = jnp.concatenate(ys[b], axis=0) * jax.nn.silu(P[b]["ssd"][:, 0:w])
        outs = []
        for grp in range(2):
            yg = y[:, grp * 128:(grp + 1) * 128]
            ms = jnp.mean(yg * yg, axis=-1, keepdims=True)
            outs.append(yg * lax.rsqrt(ms + GROUP_NORM_EPS))
        R["o_ssd"][b] = jnp.concatenate(outs, axis=1) * R["s_nw"][...]

    P_ = functools.partial
    out = []
    for b in range(nb):
        out += [P_(prep, b)] + [P_(chunk, j, b) for j in range(nch)] + [P_(finish, b)]
    return out


_MIX_IN = ("x", "nw", "w_in", "g_wup", "g_bias", "g_nw", "l_cw", "l_cb", "l_wax", "l_bax", "l_lam",
           "r_mu", "r_mus", "r_w2", "r_a2", "r_g2", "r_vec", "s_cw", "s_cb", "s_dtb", "s_alog", "s_dsk", "s_nw",
           "ones", "lvl")
_MIX_SCRATCH = ("g_st", "l_ext", "l_h", "r_cx", "r_cs", "r_h", "s_ext", "s_h")


def _mix_body(*refs, names, nb, mix):
    R = dict(zip(names, refs))
    c = pl.program_id(0)

    @pl.when(c == 0)
    def _():
        for name in ("g_st", "l_h", "r_cx", "r_cs", "r_h", "s_h"):
            R[name][...] = jnp.zeros_like(R[name])
        for b in range(nb):
            R["l_ext"][b, 0:8, :] = jnp.zeros((8, W_GROUP), f32)
            R["s_ext"][b, 0:8, :] = jnp.zeros((8, 3 * W_GROUP), f32)

    P = [dict() for _ in range(nb)]
    xn = {}

    def norm(b):
        xn[b] = _rms_norm(R["x"][b], R["nw"][...]).astype(bf16)

    def project(b, seg):
        lo, hi = _SEGS[seg]
        P[b][seg] = jnp.dot(xn[b], R["w_in"][:, lo:hi], preferred_element_type=f32)

    P_ = functools.partial
    batches = range(nb)
    rw_prep, rw_chain, rw_tail = _rw_stages(R, P, nb, mix)
    head = [P_(norm, b) for b in batches] + [P_(project, b, seg) for b in batches for seg in ("rw", "sm")]
    rest_proj = [P_(project, b, seg) for seg in ("ssd", "lru", "gla") for b in batches]
    others = _ssd_stages(R, P, nb) + _lru_stages(R, P, nb) + _gla_stages(R, P, nb)
    for stage in head + _merge(rw_prep, rest_proj) + _merge(rw_chain + rw_tail, others):
        stage()


def _mixers(x3, nw, w_in_p, layer, v_first, prm):
    nb, s, d = x3.shape
    tc = MIX_TILE
    w = W_GROUP
    mix = v_first is not None
    blk = lambda width: pl.BlockSpec((nb, tc, width), lambda c: (0, c, 0))
    full = lambda a: pl.BlockSpec(a.shape, lambda c: (0,) * a.ndim)
    names = list(_MIX_IN) + (["r_v1", "r_v2", "vf"] if mix else [])
    arrays = dict(prm, x=x3, nw=nw, w_in=w_in_p, vf=v_first)
    specs = dict(x=blk(d), vf=blk(w),
                 w_in=pl.BlockSpec((None, d, w_in_p.shape[-1]), lambda c: (layer, 0, 0),
                                   pipeline_mode=pl.Buffered(1)))
    ins = [arrays[n] for n in names]
    in_specs = [specs[n] if n in specs else full(arrays[n]) for n in names]
    out_names = ["o_gla", "o_lru", "o_rw", "o_ssd"] + ([] if mix else ["o_v"])
    out_shape = jax.ShapeDtypeStruct((nb, s, w), f32)
    scratch = [pltpu.VMEM((nb, w, GLA_DK), f32), pltpu.VMEM((nb, tc + 8, w), f32), pltpu.VMEM((nb, 1, w), f32),
               pltpu.VMEM((nb, 1, 3 * w), f32), pltpu.VMEM((nb, 1, SM_W), f32), pltpu.VMEM((nb, w, w), f32),
               pltpu.VMEM((nb, tc + 8, 3 * w), f32), pltpu.VMEM((nb, SSD_DSTATE, w), f32)]
    outs = pl.pallas_call(
        functools.partial(_mix_body, names=tuple(names + out_names) + _MIX_SCRATCH, nb=nb, mix=mix),
        grid=(s // tc,),
        in_specs=in_specs,
        out_specs=[blk(w)] * len(out_names),
        out_shape=[out_shape] * len(out_names),
        scratch_shapes=scratch,
        compiler_params=pltpu.CompilerParams(
            dimension_semantics=("arbitrary",), vmem_limit_bytes=VMEM_LIMIT),
        name="mixers",
    )(*ins)
    return outs[:4], (v_first if mix else outs[4])


def _level_masks():
    n = 4 * RW_CHUNK
    t = np.arange(n)[:, None]
    u = np.arange(n)[None, :]
    out = []
    s = 1
    while s < RW_CHUNK:
        out.append(((t // (2 * s)) == (u // (2 * s))) & ((t % (2 * s)) >= s) & ((u % (2 * s)) < s))
        s *= 2
    return jnp.asarray(np.stack(out).astype(np.float32), dtype=bf16)


def _ones_blocks(seg):
    i = np.arange(W_GROUP)
    return jnp.asarray((i[:, None] // seg == i[None, :] // seg).astype(np.float32), dtype=bf16)


def _pad_rows(m, row0, rows):
    return jnp.zeros((rows, m.shape[1]), m.dtype).at[row0:row0 + m.shape[0], :].set(m)


def _lane_row(v, lane0, width):
    return jnp.zeros((1, width), f32).at[0, lane0:lane0 + v.shape[0]].set(v)


def _block_diag(w):
    n, k, j = w.shape
    out = jnp.zeros((n * k, n * j), w.dtype)
    for i in range(n):
        out = out.at[i * k:(i + 1) * k, i * j:(i + 1) * j].set(w[i])
    return out


def kernel(x, ffn1_norm, ffn1_w_gate, ffn1_w_up, ffn1_w_down, mix_norm, w_in, w_out, gla_alpha_up, gla_alpha_bias, gla_norm, lru_conv_w, lru_conv_b, lru_w_a, lru_b_a, lru_w_x, lru_b_x, lru_lambda, rw_mu, rw_w0, rw_w2, rw_a0, rw_a2, rw_g2, rw_v0, rw_v1, rw_v2, rw_k_k, rw_k_a, rw_r_k, rw_gn_w, rw_gn_b, ssd_conv_w, ssd_conv_b, ssd_dt_bias, ssd_a_log, ssd_d, ssd_norm, ffn2_norm, ffn2_w_gate, ffn2_w_up, ffn2_w_down, final_norm):
    nb, s, d = x.shape
    t = nb * s
    depth = w_in.shape[0]
    tm = min(512, t)

    wg1, wu1, wd1 = (a.astype(bf16) for a in (ffn1_w_gate, ffn1_w_up, ffn1_w_down))
    wg2, wu2, wd2 = (a.astype(bf16) for a in (ffn2_w_gate, ffn2_w_up, ffn2_w_down))
    w_out_b = w_out.astype(bf16)
    pad = jnp.zeros((depth, d, SM_W - 84), w_in.dtype)
    w_in_p = jnp.concatenate([
        w_in[:, :, _GLA0:_GLA0 + 768], w_in[:, :, _LRU0:_LRU0 + 512], w_in[:, :, _RW0:_RW0 + 768],
        w_in[:, :, _SSD0:_SSD0 + 1024],
        w_in[:, :, _GLA0 + 768:_GLA0 + 784], w_in[:, :, _RW0 + 768:_RW0 + 832],
        w_in[:, :, _SSD0 + 1024:_SSD0 + 1028], pad], axis=-1).astype(bf16)
    ones64 = _ones_blocks(64)
    lvl = _level_masks()

    x2 = x.reshape(t, d)
    v_first = None
    for l in range(depth):
        x2 = _ffn(x2, ffn1_norm[l][None, :], wg1, wu1, wd1, l, tm=tm)

        mu = rw_mu[l]
        prm = dict(
            g_wup=_pad_rows(gla_alpha_up[l], SM_GLA, SM_W), g_bias=gla_alpha_bias[l][None, :],
            g_nw=jnp.tile(gla_norm[l], GLA_HEADS)[None, :],
            l_cw=lru_conv_w[l], l_cb=lru_conv_b[l][None, :],
            l_wax=jnp.concatenate([_block_diag(lru_w_a[l]), _block_diag(lru_w_x[l])], axis=1).astype(bf16),
            l_bax=jnp.concatenate([lru_b_a[l], lru_b_x[l]])[None, :], l_lam=lru_lambda[l][None, :],
            r_mu=mu[None, 0:768], r_mus=_lane_row(mu[768:832], SM_RW, SM_W),
            r_w2=_pad_rows(rw_w2[l], SM_RW, SM_W), r_a2=_pad_rows(rw_a2[l], SM_RW + 16, SM_W),
            r_g2=_pad_rows(rw_g2[l], SM_RW + 32, SM_W),
            r_vec=jnp.stack([rw_w0[l], rw_a0[l], rw_k_k[l], rw_k_a[l], rw_r_k[l].reshape(-1), rw_gn_w[l],
                             rw_gn_b[l], rw_v0[l - 1] if l > 0 else jnp.zeros_like(rw_w0[l])]),
            s_cw=ssd_conv_w[l], s_cb=ssd_conv_b[l][None, :], s_dtb=_lane_row(ssd_dt_bias[l], SM_DT, SM_W),
            s_alog=jnp.concatenate([_lane_row(ssd_a_log[l], SM_DT, SM_W),
                                    _lane_row(jnp.ones((SSD_HEADS,), f32), SM_DT, SM_W)]),
            s_dsk=jnp.repeat(ssd_d[l], W_GROUP // SSD_HEADS)[None, :], s_nw=ssd_norm[l][None, :],
            ones=ones64, lvl=lvl)
        if l > 0:
            prm["r_v1"] = jnp.zeros((W_GROUP, 128), f32).at[:, 0:rw_v1.shape[-1]].set(rw_v1[l - 1])
            prm["r_v2"] = _pad_rows(rw_v2[l - 1], 0, 128)
        ys, v_first = _mixers(x2.reshape(nb, s, d), mix_norm[l][None, :], w_in_p, l, v_first, prm)

        ys = [a.reshape(t, W_GROUP) for a in ys]
        x2 = _ffn(x2, ffn2_norm[l][None, :], wg2, wu2, wd2, l, tm=tm, ys=ys, w_out=w_out_b,
                  final_w=final_norm[None, :] if l == depth - 1 else None)
    return x2.reshape(nb, s, d)
```

```python
import functools
import math

import numpy as np
import jax
import jax.numpy as jnp
from jax import lax
from jax.experimental import pallas as pl
from jax.experimental.pallas import tpu as pltpu

f32 = jnp.float32
bf16 = jnp.bfloat16

D_MODEL = 1024
D_FF = 2816
W_GROUP = 256
NORM_EPS = 1e-6
GROUP_NORM_EPS = 1e-5
CONV_W = 4

GLA_HEADS = 4
GLA_DK = 128
GLA_HK = 32
GLA_HV = 64
GLA_GATE_NORM = 16.0
GLA_CHUNK = 64

LRU_C = 8.0

RW_HS = 64
RW_DECAY_SCALE = math.exp(-0.5)
RW_GN_EPS = 64e-5
RW_CHUNK = 64

SSD_HEADS = 4
SSD_DSTATE = 128
SSD_CHUNK = 128

_GLA0, _LRU0, _RW0, _SSD0 = 0, 784, 1296, 2128
_SEGS = dict(gla=(0, 768), lru=(768, 1280), rw=(1280, 2048), ssd=(2048, 3072), sm=(3072, 3200))
SM_GLA = 0
SM_RW = 16
SM_DT = 80
SM_W = 128

MIX_TILE = 256
MXU_TILE = 256
FFN_SPLITS = (MXU_TILE,) * (D_FF // MXU_TILE)
VMEM_LIMIT = 56 * 1024 * 1024


def _dot(a, b):
    return jnp.dot(a.astype(bf16), b.astype(bf16), preferred_element_type=f32)


def _dot_nt(a, b):
    return lax.dot_general(a.astype(bf16), b.astype(bf16), (((1,), (1,)), ((), ())),
                           preferred_element_type=f32)


def _dot_tn(a, b):
    return lax.dot_general(a.astype(bf16), b.astype(bf16), (((0,), (0,)), ((), ())),
                           preferred_element_type=f32)


def _seg_dot(x, ones_bf):
    return jnp.dot(x.astype(bf16), ones_bf, preferred_element_type=f32)


def _cumsum_rows(x, seg):
    row = lax.broadcasted_iota(jnp.int32, x.shape, 0) % seg
    s = 1
    while s < seg:
        x = x + jnp.where(row >= s, pltpu.roll(x, s, 0), 0.0)
        s *= 2
    return x


def _head_masks(rows, width):
    lane = lax.broadcasted_iota(jnp.int32, (rows, width), 1) // (width // 4)
    return [(lane == h).astype(f32).astype(bf16) for h in range(4)]


def _stack_heads(x, masks):
    xb = x.astype(bf16)
    return jnp.concatenate([xb * m for m in masks], axis=0)


def _expand_heads(x, lane0, width):
    n = x.shape[0]
    lane = lax.broadcasted_iota(jnp.int32, (n, width), 1) // (width // 4)
    out = jnp.zeros((n, width), f32)
    for h in range(4):
        out = jnp.where(lane == h, x[:, lane0 + h:lane0 + h + 1], out)
    return out


def _rms_norm(x, w):
    ms = jnp.mean(x * x, axis=-1, keepdims=True)
    return x * lax.rsqrt(ms + NORM_EPS) * w


def _merge(*stage_lists):
    keyed = []
    for k, stages in enumerate(stage_lists):
        keyed += [((i + 0.5) / len(stages), k, i, st) for i, st in enumerate(stages)]
    return [st for _, _, _, st in sorted(keyed, key=lambda t: t[:3])]


def _ffn_body(*refs, splits, proj, final):
    refs = list(refs)
    x_ref = refs.pop(0)
    if proj:
        y_refs = [refs.pop(0) for _ in range(4)]
        wo_ref = refs.pop(0)
    nw_ref, wg_ref, wu_ref, wd_ref = (refs.pop(0) for _ in range(4))
    if final:
        fw_ref = refs.pop(0)
    o_ref = refs.pop(0)

    x = x_ref[...]
    if proj:
        for i, y_ref in enumerate(y_refs):
            x = x + jnp.dot(y_ref[...].astype(bf16), wo_ref[i * W_GROUP:(i + 1) * W_GROUP, :],
                            preferred_element_type=f32)
    xn = _rms_norm(x, nw_ref[...]).astype(bf16)
    acc = None
    lo = 0
    for width in splits:
        hi = lo + width
        g = jnp.dot(xn, wg_ref[:, lo:hi], preferred_element_type=f32)
        u = jnp.dot(xn, wu_ref[:, lo:hi], preferred_element_type=f32)
        h = (jax.nn.silu(g) * u).astype(bf16)
        part = jnp.dot(h, wd_ref[lo:hi, :], preferred_element_type=f32)
        acc = part if acc is None else acc + part
        lo = hi
    out = x + 0.5 * acc
    if final:
        out = _rms_norm(out, fw_ref[...])
    o_ref[...] = out


def _ffn(x2, nw, wg, wu, wd, layer, *, tm, ys=None, w_out=None, final_w=None):
    t, d = x2.shape
    f = wg.shape[-1]
    proj = ys is not None
    final = final_w is not None
    tile = lambda width: pl.BlockSpec((tm, width), lambda i: (i, 0))
    row = pl.BlockSpec((1, d), lambda i: (0, 0))
    resident = lambda r, c: pl.BlockSpec((None, r, c), lambda i: (layer, 0, 0), pipeline_mode=pl.Buffered(1))
    ins = [x2] + (list(ys) + [w_out] if proj else []) + [nw, wg, wu, wd] + ([final_w] if final else [])
    in_specs = [tile(d)] + ([tile(W_GROUP)] * 4 + [resident(d, d)] if proj else []) \
        + [row, resident(d, f), resident(d, f), resident(f, d)] + ([row] if final else [])
    return pl.pallas_call(
        functools.partial(_ffn_body, splits=FFN_SPLITS, proj=proj, final=final),
        grid=(t // tm,),
        in_specs=in_specs,
        out_specs=tile(d),
        out_shape=jax.ShapeDtypeStruct((t, d), f32),
        compiler_params=pltpu.CompilerParams(
            dimension_semantics=("parallel",), vmem_limit_bytes=VMEM_LIMIT),
        name="ffn",
    )(*ins)


def _gla_stages(R, P, nb):
    L = GLA_CHUNK
    tc = MIX_TILE
    nch = tc // L
    row = lax.broadcasted_iota(jnp.int32, (L, 4 * L), 0)
    col = lax.broadcasted_iota(jnp.int32, (L, 4 * L), 1)
    causal = (col % L) <= row
    srow = lax.broadcasted_iota(jnp.int32, (W_GROUP, GLA_DK), 0) // GLA_HV
    scol = lax.broadcasted_iota(jnp.int32, (W_GROUP, GLA_DK), 1) // GLA_HK
    blockdiag = srow == scol
    k_masks = _head_masks(L, GLA_DK)
    v_masks = _head_masks(L, W_GROUP)
    pre, intra, kv, dec = {}, {}, {}, {}
    st = {}
    outs = [[None] * nch for _ in range(nb)]

    def prep(b):
        x = P[b]["gla"]
        z = _dot(P[b]["sm"], R["g_wup"][...]) + R["g_bias"][...]
        bc = _cumsum_rows(jax.nn.log_sigmoid(z) / GLA_GATE_NORM, L)
        q_dec = (x[:, 0:128] * (GLA_HK ** -0.5) * jnp.exp(bc)).astype(bf16)
        pre[b] = (x, bc, q_dec, x[:, 128:256] * jnp.exp(-bc))
        st[b] = R["g_st"][b]

    def chain(j, b):
        x, bc, q_dec, k_dec = pre[b]
        rows = slice(j * L, (j + 1) * L)
        s_cat = _dot_nt(q_dec[rows], _stack_heads(k_dec[rows], k_masks))
        s_cat = jnp.where(causal, s_cat, 0.0)
        v = x[rows, 256:512]
        intra[j, b] = _dot(s_cat, _stack_heads(v, v_masks))
        b_last = bc[(j + 1) * L - 1:(j + 1) * L, :]
        k_end = x[rows, 128:256] * jnp.exp(b_last - bc[rows])
        kv[j, b] = jnp.where(blockdiag, _dot_tn(v, k_end), 0.0)
        dec[j, b] = jnp.exp(b_last)

    def link(j, b):
        q_dec = pre[b][2]
        outs[b][j] = intra[j, b] + _dot_nt(q_dec[j * L:(j + 1) * L], st[b])
        st[b] = st[b] * dec[j, b] + kv[j, b]

    def finish(b):
        R["g_st"][b] = st[b]
        o = jnp.concatenate(outs[b], axis=0)
        ms = _seg_dot(o * o, R["ones"][...]) * (1.0 / GLA_HV)
        o = o * lax.rsqrt(ms + GROUP_NORM_EPS) * R["g_nw"][...]
        R["o_gla"][b] = o * jax.nn.silu(pre[b][0][:, 512:768])

    P_ = functools.partial
    chains = [(j, b) for j in range(nch) for b in range(nb)]
    return ([P_(prep, b) for b in range(nb)] + [P_(chain, j, b) for j, b in chains]
            + [P_(link, j, b) for j, b in chains] + [P_(finish, b) for b in range(nb)])


def _lru_stages(R, P, nb):
    tc = MIX_TILE
    w = W_GROUP
    row = lax.broadcasted_iota(jnp.int32, (tc, w), 0)
    au = {}

    def gates(b):
        x = P[b]["lru"]
        xb = x[:, 0:w]
        ext = R["l_ext"]
        ext[b, 8:, :] = xb
        xc = R["l_cb"][...] + R["l_cw"][CONV_W - 1:CONV_W, :] * xb
        for j in range(1, CONV_W):
            xc = xc + R["l_cw"][CONV_W - 1 - j:CONV_W - j, :] * ext[b, pl.ds(8 - j, tc), :]
        ext[b, 0:8, :] = xb[tc - 8:tc, :]
        ri = jax.nn.sigmoid(_dot(xc, R["l_wax"][...]) + R["l_bax"][...])
        log_a = -LRU_C * ri[:, 0:w] * jax.nn.softplus(-R["l_lam"][...])
        a = jnp.exp(log_a)
        au[b] = (a, jnp.sqrt(1.0 - a * a) * (ri[:, w:2 * w] * xc))

    def scan(b, shifts):
        a, u = au[b]
        for s in shifts:
            keep = row >= s
            u = u + a * jnp.where(keep, pltpu.roll(u, s, 0), 0.0)
            a = a * jnp.where(keep, pltpu.roll(a, s, 0), 1.0)
        au[b] = (a, u)

    def finish(b):
        a, u = au[b]
        h = u + a * R["l_h"][b]
        R["l_h"][b] = h[tc - 1:tc, :]
        R["o_lru"][b] = h * jax.nn.gelu(P[b]["lru"][:, w:2 * w])

    shifts = [1 << i for i in range(tc.bit_length() - 1)]
    P_ = functools.partial
    out = []
    for b in range(nb):
        out += [P_(gates, b)] + [P_(scan, b, shifts[i:i + 2]) for i in range(0, len(shifts), 2)] + [P_(finish, b)]
    return out


def _rw_stages(R, P, nb, mix):
    L = RW_CHUNK
    tc = MIX_TILE
    nch = tc // L
    w = W_GROUP
    n4 = 4 * L
    row3 = lax.broadcasted_iota(jnp.int32, (tc, 3 * w), 0)
    rows_sm = lax.broadcasted_iota(jnp.int32, (tc, SM_W), 0)
    rr = lax.broadcasted_iota(jnp.int32, (n4, n4), 0)
    cc = lax.broadcasted_iota(jnp.int32, (n4, n4), 1)
    strict = (rr % L) > (cc % L)
    incl = (rr % L) >= (cc % L)
    eye = (rr == cc).astype(f32).astype(bf16)
    masks = _head_masks(L, w)
    w0, a0, k_k, k_a, r_k, gn_w, gn_b, v0 = (R["r_vec"][i:i + 1, :] for i in range(8))
    lvl = R["lvl"]
    pre, hs = {}, {}
    ch = {}
    ys = [[None] * nch for _ in range(nb)]
    keys = [(j, b) for j in range(nch) for b in range(nb)]

    def prep_a(b):
        x = P[b]["rw"]
        sm = P[b]["sm"]
        x_prev = jnp.where(row3 == 0, R["r_cx"][b], pltpu.roll(x, 1, 0))
        sm_prev = jnp.where(rows_sm == 0, R["r_cs"][b], pltpu.roll(sm, 1, 0))
        R["r_cx"][b] = x[tc - 1:tc, :]
        R["r_cs"][b] = sm[tc - 1:tc, :]
        x = x + (x_prev - x) * R["r_mu"][...]
        sm = sm + (sm_prev - sm) * R["r_mus"][...]
        r = x[:, 0:w]
        k = x[:, w:2 * w]
        v = x[:, 2 * w:3 * w]
        log_w = -RW_DECAY_SCALE * jax.nn.sigmoid(w0 + _dot(jnp.tanh(sm), R["r_w2"][...]))
        a = jax.nn.sigmoid(a0 + _dot(sm, R["r_a2"][...]))
        g = _dot(jax.nn.sigmoid(sm), R["r_g2"][...])
        if mix:
            lam = jax.nn.sigmoid(v0 + _dot(_dot(v, R["r_v1"][...]), R["r_v2"][...]))
            v = v + (R["vf"][b] - v) * lam
        else:
            R["o_v"][b] = v
        pre[b] = dict(r=r, k=k, v=v, g=g, a=a, log_w=log_w)

    def prep_b(b):
        p = pre[b]
        r, k, v, g, a, log_w = (p[n] for n in ("r", "k", "v", "g", "a", "log_w"))
        kk = k * k_k
        n2 = _seg_dot(kk * kk, R["ones"][...])
        kk = kk / jnp.maximum(jnp.sqrt(n2), 1e-12)
        k = k * (1.0 + (a - 1.0) * k_a)
        cum = _cumsum_rows(log_w, L)
        g_in = jnp.exp(cum)
        g_inv = jnp.exp(-cum)
        pre[b] = dict(r=r, k=k, v=v, g=g, g_in=g_in,
                      al=-kk * jnp.exp(cum - log_w),
                      be=kk * a * g_inv, kt=k * g_inv, rt=r * g_in)
        hs[b] = R["r_h"][b]

    def scores(j, b):
        p = pre[b]
        rows = slice(j * L, (j + 1) * L)
        g_last = p["g_in"][(j + 1) * L - 1:(j + 1) * L, :]
        d = dict()
        d["g_col"] = jnp.broadcast_to(jnp.broadcast_to(g_last, (8, w)).T[:, 0:1], (w, w))
        d["al"] = _stack_heads(p["al"][rows], masks)
        d["rt"] = _stack_heads(p["rt"][rows], masks)
        d["vst"] = _stack_heads(p["v"][rows], masks)
        be = _stack_heads(p["be"][rows], masks)
        kt = _stack_heads(p["kt"][rows], masks)
        d["be_end"] = _stack_heads(p["be"][rows] * g_last, masks)
        kt_end = _stack_heads(p["kt"][rows] * g_last, masks)
        aa = _dot_nt(jnp.concatenate([d["al"], d["rt"]], axis=0), jnp.concatenate([be, kt], axis=0))
        d["a_ab"] = jnp.where(strict, aa[0:n4, 0:n4], 0.0).astype(bf16)
        d["a_ak"] = jnp.where(strict, aa[0:n4, n4:2 * n4], 0.0).astype(bf16)
        d["a_rb"] = jnp.where(incl, aa[n4:2 * n4, 0:n4], 0.0).astype(bf16)
        d["a_rk"] = jnp.where(incl, aa[n4:2 * n4, n4:2 * n4], 0.0).astype(bf16)
        d["kv"] = _dot_tn(kt_end, d["vst"])
        ch[j, b] = d

    def level0():
        for key in keys:
            ch[key]["t"] = eye + ch[key]["a_ab"] * lvl[0]

    def level_a(lv, part):
        for key in part:
            ch[key]["x1"] = _dot(ch[key]["t"], ch[key]["a_ab"] * lvl[lv])

    def level_b(part):
        for key in part:
            ch[key]["t"] = ch[key]["t"] + _dot(ch[key]["x1"], ch[key]["t"]).astype(bf16)

    def solve(j, b):
        d = ch[j, b]
        ak = _dot(jnp.concatenate([d["a_ak"], d["a_rk"]], axis=0), d["vst"])
        pq = _dot(d["t"], jnp.concatenate([d["al"], ak[0:n4].astype(bf16)], axis=1))
        d["p_st"] = pq[:, 0:w].astype(bf16)
        d["q_st"] = pq[:, w:2 * w]
        d["y_k"] = ak[n4:2 * n4]

    def link(j, b):
        d = ch[j, b]
        h_bf = hs[b].astype(bf16)
        u = (_dot(d["p_st"], h_bf) + d["q_st"]).astype(bf16)
        hs[b] = hs[b] * d["g_col"] + _dot_tn(d["be_end"], u) + d["kv"]
        y_st = _dot(d["rt"], h_bf) + _dot(d["a_rb"], u) + d["y_k"]
        ys[b][j] = y_st[0:L] + y_st[L:2 * L] + y_st[2 * L:3 * L] + y_st[3 * L:4 * L]

    def finish(b):
        R["r_h"][b] = hs[b]
        p = pre[b]
        y = jnp.concatenate(ys[b], axis=0)
        mean = _seg_dot(y, R["ones"][...]) * (1.0 / RW_HS)
        dlt = y - mean
        var = _seg_dot(dlt * dlt, R["ones"][...]) * (1.0 / RW_HS)
        yn = dlt * lax.rsqrt(var + RW_GN_EPS) * gn_w + gn_b
        bonus = _seg_dot(p["r"] * p["k"] * r_k, R["ones"][...]) * p["v"]
        R["o_rw"][b] = (yn + bonus) * p["g"]

    P_ = functools.partial
    n_lvl = RW_CHUNK.bit_length() - 1
    halves = (keys[:len(keys) // 2], keys[len(keys) // 2:])
    return ([st for b in range(nb) for st in (P_(prep_a, b), P_(prep_b, b))],
            [P_(scores, j, b) for j, b in keys] + [level0]
            + [st for lv in range(1, n_lvl) for st in
               (P_(level_a, lv, halves[0]), P_(level_a, lv, halves[1]), P_(level_b, halves[0]), P_(level_b, halves[1]))]
            + [P_(solve, j, b) for j, b in keys],
            [P_(link, j, b) for j, b in keys] + [P_(finish, b) for b in range(nb)])


def _ssd_stages(R, P, nb):
    L = SSD_CHUNK
    tc = MIX_TILE
    nch = tc // L
    w = W_GROUP
    cdim = 3 * w
    ll = lax.broadcasted_iota(jnp.int32, (L, L), 0)
    ss = lax.broadcasted_iota(jnp.int32, (L, L), 1)
    causal = ll >= ss
    lane_head = lax.broadcasted_iota(jnp.int32, (L, w), 1) // (w // SSD_HEADS)
    pre, hst = {}, {}
    ys = [[None] * nch for _ in range(nb)]

    def prep(b):
        x = P[b]["ssd"]
        xbc = x[:, w:4 * w]
        ext = R["s_ext"]
        ext[b, 8:, :] = xbc
        conv = R["s_cb"][...] + R["s_cw"][CONV_W - 1:CONV_W, :] * xbc
        for j in range(1, CONV_W):
            conv = conv + R["s_cw"][CONV_W - 1 - j:CONV_W - j, :] * ext[b, pl.ds(8 - j, tc), :]
        ext[b, 0:8, :] = xbc[tc - 8:tc, :]
        xbc = jax.nn.silu(conv)
        dt = jax.nn.softplus(P[b]["sm"] + R["s_dtb"][...])
        d_a = dt * (-jnp.exp(R["s_alog"][0:1, :]) * R["s_alog"][1:2, :])
        cs = _cumsum_rows(d_a, L)
        pre[b] = dict(xbc=xbc, cs=cs, dt_full=_expand_heads(dt, SM_DT, w), cs_full=_expand_heads(cs, SM_DT, w))
        hst[b] = R["s_h"][b]

    cur = {}

    def chunk_pre(j, b):
        p = pre[b]
        rows = slice(j * L, (j + 1) * L)
        xs = p["xbc"][rows, 0:w]
        cs = p["cs"][rows]
        cs_full = p["cs_full"][rows]
        cs_last = cs_full[L - 1:L, :]
        xdt = xs * p["dt_full"][rows]
        cur[b] = dict(xs=xs, bm=p["xbc"][rows, w:2 * w], cm=p["xbc"][rows, 2 * w:3 * w], cs=cs, cs_t=cs.T,
                      cs_full=cs_full, cs_last=cs_last, xdt=xdt, xw=xdt * jnp.exp(cs_last - cs_full),
                      y=jnp.zeros((L, w), f32), y_off=[], st=[])
        cur[b]["bm_t"] = cur[b]["bm"].T

    def group(b, grp):
        q = cur[b]
        lo, hi = grp * SSD_DSTATE, (grp + 1) * SSD_DSTATE
        cb_g = _dot_nt(q["cm"][:, lo:hi], q["bm"][:, lo:hi])
        for r in range(2):
            hd = grp * 2 + r
            seg = q["cs"][:, SM_DT + hd:SM_DT + hd + 1] - q["cs_t"][SM_DT + hd:SM_DT + hd + 1, :]
            m = cb_g * jnp.where(causal, jnp.exp(seg), 0.0)
            q["y"] = q["y"] + _dot(m, jnp.where(lane_head == hd, q["xdt"], 0.0))
        q["y_off"].append(_dot(q["cm"][:, lo:hi], hst[b][:, lo:hi]))
        q["st"].append(_dot(q["bm_t"][lo:hi, :], q["xw"][:, lo:hi]))

    def chunk_fin(j, b):
        q = cur[b]
        y = q["y"] + jnp.concatenate(q["y_off"], axis=1) * jnp.exp(q["cs_full"])
        hst[b] = hst[b] * jnp.exp(q["cs_last"]) + jnp.concatenate(q["st"], axis=1)
        ys[b][j] = y + R["s_dsk"][...] * q["xs"]

    def finish(b):
        R["s_h"][b] = hst[b]
        y = jnp.concatenate(ys[b], axis=0) * jax.nn.silu(P[b]["ssd"][:, 0:w])
        outs = []
        for grp in range(2):
            yg = y[:, grp * 128:(grp + 1) * 128]
            ms = jnp.mean(yg * yg, axis=-1, keepdims=True)
            outs.append(yg * lax.rsqrt(ms + GROUP_NORM_EPS))
        R["o_ssd"][b] = jnp.concatenate(outs, axis=1) * R["s_nw"][...]

    P_ = functools.partial
    out = []
    for b in range(nb):
        out += [P_(prep, b)]
        for j in range(nch):
            out += [P_(chunk_pre, j, b), P_(group, b, 0), P_(group, b, 1), P_(chunk_fin, j, b)]
        out += [P_(finish, b)]
    return out


_MIX_IN = ("x", "nw", "w_in", "g_wup", "g_bias", "g_nw", "l_cw", "l_cb", "l_wax", "l_bax", "l_lam",
           "r_mu", "r_mus", "r_w2", "r_a2", "r_g2", "r_vec", "s_cw", "s_cb", "s_dtb", "s_alog", "s_dsk", "s_nw",
           "ones", "lvl")
_MIX_SCRATCH = ("g_st", "l_ext", "l_h", "r_cx", "r_cs", "r_h", "s_ext", "s_h")


def _mix_body(*refs, names, nb, mix):
    R = dict(zip(names, refs))
    c = pl.program_id(0)

    @pl.when(c == 0)
    def _():
        for name in ("g_st", "l_h", "r_cx", "r_cs", "r_h", "s_h"):
            R[name][...] = jnp.zeros_like(R[name])
        for b in range(nb):
            R["l_ext"][b, 0:8, :] = jnp.zeros((8, W_GROUP), f32)
            R["s_ext"][b, 0:8, :] = jnp.zeros((8, 3 * W_GROUP), f32)

    P = [dict() for _ in range(nb)]
    xn = {}

    def norm(b):
        xn[b] = _rms_norm(R["x"][b], R["nw"][...]).astype(bf16)

    def project(b, seg):
        lo, hi = _SEGS[seg]
        P[b][seg] = jnp.dot(xn[b], R["w_in"][:, lo:hi], preferred_element_type=f32)

    P_ = functools.partial
    batches = range(nb)
    rw_prep, rw_chain, rw_tail = _rw_stages(R, P, nb, mix)
    head = [P_(norm, b) for b in batches] + [P_(project, b, seg) for b in batches for seg in ("rw", "sm")]
    rest_proj = [P_(project, b, seg) for seg in ("ssd", "lru", "gla") for b in batches]
    others = _ssd_stages(R, P, nb) + _lru_stages(R, P, nb) + _gla_stages(R, P, nb)
    for stage in head + _merge(rw_prep, rest_proj) + _merge(rw_chain + rw_tail, others):
        stage()


def _mixers(x3, nw, w_in_p, layer, v_first, prm):
    nb, s, d = x3.shape
    tc = MIX_TILE
    w = W_GROUP
    mix = v_first is not None
    blk = lambda width: pl.BlockSpec((nb, tc, width), lambda c: (0, c, 0))
    full = lambda a: pl.BlockSpec(a.shape, lambda c: (0,) * a.ndim)
    names = list(_MIX_IN) + (["r_v1", "r_v2", "vf"] if mix else [])
    arrays = dict(prm, x=x3, nw=nw, w_in=w_in_p, vf=v_first)
    specs = dict(x=blk(d), vf=blk(w),
                 w_in=pl.BlockSpec((None, d, w_in_p.shape[-1]), lambda c: (layer, 0, 0),
                                   pipeline_mode=pl.Buffered(1)))
    ins = [arrays[n] for n in names]
    in_specs = [specs[n] if n in specs else full(arrays[n]) for n in names]
    out_names = ["o_gla", "o_lru", "o_rw", "o_ssd"] + ([] if mix else ["o_v"])
    out_shape = jax.ShapeDtypeStruct((nb, s, w), f32)
    scratch = [pltpu.VMEM((nb, w, GLA_DK), f32), pltpu.VMEM((nb, tc + 8, w), f32), pltpu.VMEM((nb, 1, w), f32),
               pltpu.VMEM((nb, 1, 3 * w), f32), pltpu.VMEM((nb, 1, SM_W), f32), pltpu.VMEM((nb, w, w), f32),
               pltpu.VMEM((nb, tc + 8, 3 * w), f32), pltpu.VMEM((nb, SSD_DSTATE, w), f32)]
    outs = pl.pallas_call(
        functools.partial(_mix_body, names=tuple(names + out_names) + _MIX_SCRATCH, nb=nb, mix=mix),
        grid=(s // tc,),
        in_specs=in_specs,
        out_specs=[blk(w)] * len(out_names),
        out_shape=[out_shape] * len(out_names),
        scratch_shapes=scratch,
        compiler_params=pltpu.CompilerParams(
            dimension_semantics=("arbitrary",), vmem_limit_bytes=VMEM_LIMIT),
        name="mixers",
    )(*ins)
    return outs[:4], (v_first if mix else outs[4])


def _level_masks():
    n = 4 * RW_CHUNK
    t = np.arange(n)[:, None]
    u = np.arange(n)[None, :]
    out = []
    s = 1
    while s < RW_CHUNK:
        out.append(((t // (2 * s)) == (u // (2 * s))) & ((t % (2 * s)) >= s) & ((u % (2 * s)) < s))
        s *= 2
    return jnp.asarray(np.stack(out).astype(np.float32), dtype=bf16)


def _ones_blocks(seg):
    i = np.arange(W_GROUP)
    return jnp.asarray((i[:, None] // seg == i[None, :] // seg).astype(np.float32), dtype=bf16)


def _pad_rows(m, row0, rows):
    return jnp.zeros((rows, m.shape[1]), m.dtype).at[row0:row0 + m.shape[0], :].set(m)


def _lane_row(v, lane0, width):
    return jnp.zeros((1, width), f32).at[0, lane0:lane0 + v.shape[0]].set(v)


def _block_diag(w):
    n, k, j = w.shape
    out = jnp.zeros((n * k, n * j), w.dtype)
    for i in range(n):
        out = out.at[i * k:(i + 1) * k, i * j:(i + 1) * j].set(w[i])
    return out


def kernel(x, ffn1_norm, ffn1_w_gate, ffn1_w_up, ffn1_w_down, mix_norm, w_in, w_out, gla_alpha_up, gla_alpha_bias, gla_norm, lru_conv_w, lru_conv_b, lru_w_a, lru_b_a, lru_w_x, lru_b_x, lru_lambda, rw_mu, rw_w0, rw_w2, rw_a0, rw_a2, rw_g2, rw_v0, rw_v1, rw_v2, rw_k_k, rw_k_a, rw_r_k, rw_gn_w, rw_gn_b, ssd_conv_w, ssd_conv_b, ssd_dt_bias, ssd_a_log, ssd_d, ssd_norm, ffn2_norm, ffn2_w_gate, ffn2_w_up, ffn2_w_down, final_norm):
    nb, s, d = x.shape
    t = nb * s
    depth = w_in.shape[0]
    tm = min(512, t)

    wg1, wu1, wd1 = (a.astype(bf16) for a in (ffn1_w_gate, ffn1_w_up, ffn1_w_down))
    wg2, wu2, wd2 = (a.astype(bf16) for a in (ffn2_w_gate, ffn2_w_up, ffn2_w_down))
    w_out_b = w_out.astype(bf16)
    pad = jnp.zeros((depth, d, SM_W - 84), w_in.dtype)
    w_in_p = jnp.concatenate([
        w_in[:, :, _GLA0:_GLA0 + 768], w_in[:, :, _LRU0:_LRU0 + 512], w_in[:, :, _RW0:_RW0 + 768],
        w_in[:, :, _SSD0:_SSD0 + 1024],
        w_in[:, :, _GLA0 + 768:_GLA0 + 784], w_in[:, :, _RW0 + 768:_RW0 + 832],
        w_in[:, :, _SSD0 + 1024:_SSD0 + 1028], pad], axis=-1).astype(bf16)
    ones64 = _ones_blocks(64)
    lvl = _level_masks()

    x2 = x.reshape(t, d)
    v_first = None
    for l in range(depth):
        x2 = _ffn(x2, ffn1_norm[l][None, :], wg1, wu1, wd1, l, tm=tm)

        mu = rw_mu[l]
        prm = dict(
            g_wup=_pad_rows(gla_alpha_up[l], SM_GLA, SM_W), g_bias=gla_alpha_bias[l][None, :],
            g_nw=jnp.tile(gla_norm[l], GLA_HEADS)[None, :],
            l_cw=lru_conv_w[l], l_cb=lru_conv_b[l][None, :],
            l_wax=jnp.concatenate([_block_diag(lru_w_a[l]), _block_diag(lru_w_x[l])], axis=1).astype(bf16),
            l_bax=jnp.concatenate([lru_b_a[l], lru_b_x[l]])[None, :], l_lam=lru_lambda[l][None, :],
            r_mu=mu[None, 0:768], r_mus=_lane_row(mu[768:832], SM_RW, SM_W),
            r_w2=_pad_rows(rw_w2[l], SM_RW, SM_W), r_a2=_pad_rows(rw_a2[l], SM_RW + 16, SM_W),
            r_g2=_pad_rows(rw_g2[l], SM_RW + 32, SM_W),
            r_vec=jnp.stack([rw_w0[l], rw_a0[l], rw_k_k[l], rw_k_a[l], rw_r_k[l].reshape(-1), rw_gn_w[l],
                             rw_gn_b[l], rw_v0[l - 1] if l > 0 else jnp.zeros_like(rw_w0[l])]),
            s_cw=ssd_conv_w[l], s_cb=ssd_conv_b[l][None, :], s_dtb=_lane_row(ssd_dt_bias[l], SM_DT, SM_W),
            s_alog=jnp.concatenate([_lane_row(ssd_a_log[l], SM_DT, SM_W),
                                    _lane_row(jnp.ones((SSD_HEADS,), f32), SM_DT, SM_W)]),
            s_dsk=jnp.repeat(ssd_d[l], W_GROUP // SSD_HEADS)[None, :], s_nw=ssd_norm[l][None, :],
            ones=ones64, lvl=lvl)
        if l > 0:
            prm["r_v1"] = jnp.zeros((W_GROUP, 128), f32).at[:, 0:rw_v1.shape[-1]].set(rw_v1[l - 1])
            prm["r_v2"] = _pad_rows(rw_v2[l - 1], 0, 128)
        ys, v_first = _mixers(x2.reshape(nb, s, d), mix_norm[l][None, :], w_in_p, l, v_first, prm)

        ys = [a.reshape(t, W_GROUP) for a in ys]
        x2 = _ffn(x2, ffn2_norm[l][None, :], wg2, wu2, wd2, l, tm=tm, ys=ys, w_out=w_out_b,
                  final_w=final_norm[None, :] if l == depth - 1 else None)
    return x2.reshape(nb, s, d)
```

```python
import functools
import math

import numpy as np
import jax
import jax.numpy as jnp
from jax import lax
from jax.experimental import pallas as pl
from jax.experimental.pallas import tpu as pltpu

f32 = jnp.float32
bf16 = jnp.bfloat16

D_MODEL = 1024
D_FF = 2816
W_GROUP = 256
NORM_EPS = 1e-6
GROUP_NORM_EPS = 1e-5
CONV_W = 4

GLA_HEADS = 4
GLA_DK = 128
GLA_HK = 32
GLA_HV = 64
GLA_GATE_NORM = 16.0
GLA_CHUNK = 64

LRU_C = 8.0

RW_HS = 64
RW_DECAY_SCALE = math.exp(-0.5)
RW_GN_EPS = 64e-5
RW_CHUNK = 64

SSD_HEADS = 4
SSD_DSTATE = 128
SSD_CHUNK = 128

_GLA0, _LRU0, _RW0, _SSD0 = 0, 784, 1296, 2128
_SEGS = dict(gla=(0, 768), lru=(768, 1280), rw=(1280, 2048), ssd=(2048, 3072), sm=(3072, 3200))
SM_GLA = 0
SM_RW = 16
SM_DT = 80
SM_W = 128

MIX_TILE = 256
MXU_TILE = 256
FFN_SPLITS = (MXU_TILE,) * (D_FF // MXU_TILE)
VMEM_LIMIT = 56 * 1024 * 1024


def _dot(a, b):
    return jnp.dot(a.astype(bf16), b.astype(bf16), preferred_element_type=f32)


def _dot_nt(a, b):
    return lax.dot_general(a.astype(bf16), b.astype(bf16), (((1,), (1,)), ((), ())),
                           preferred_element_type=f32)


def _dot_tn(a, b):
    return lax.dot_general(a.astype(bf16), b.astype(bf16), (((0,), (0,)), ((), ())),
                           preferred_element_type=f32)


def _seg_dot(x, ones_bf):
    return jnp.dot(x.astype(bf16), ones_bf, preferred_element_type=f32)


def _cumsum_rows(x, seg):
    row = lax.broadcasted_iota(jnp.int32, x.shape, 0) % seg
    s = 1
    while s < seg:
        x = x + jnp.where(row >= s, pltpu.roll(x, s, 0), 0.0)
        s *= 2
    return x


def _head_masks(rows, width):
    lane = lax.broadcasted_iota(jnp.int32, (rows, width), 1) // (width // 4)
    return [(lane == h).astype(f32).astype(bf16) for h in range(4)]


def _stack_heads(x, masks):
    xb = x.astype(bf16)
    return jnp.concatenate([xb * m for m in masks], axis=0)


def _expand_heads(x, lane0, width):
    n = x.shape[0]
    lane = lax.broadcasted_iota(jnp.int32, (n, width), 1) // (width // 4)
    out = jnp.zeros((n, width), f32)
    for h in range(4):
        out = jnp.where(lane == h, x[:, lane0 + h:lane0 + h + 1], out)
    return out


def _rms_norm(x, w):
    ms = jnp.mean(x * x, axis=-1, keepdims=True)
    return x * lax.rsqrt(ms + NORM_EPS) * w


def _merge(*stage_lists):
    keyed = []
    for k, stages in enumerate(stage_lists):
        total = float(sum(c for c, _ in stages))
        done = 0.0
        for i, (c, st) in enumerate(stages):
            keyed.append(((done + 0.5 * c) / total, k, i, st))
            done += c
    return [st for _, _, _, st in sorted(keyed, key=lambda t: t[:3])]


def _ffn_body(*refs, splits, proj, final):
    refs = list(refs)
    x_ref = refs.pop(0)
    if proj:
        y_refs = [refs.pop(0) for _ in range(4)]
        wo_ref = refs.pop(0)
    nw_ref, wg_ref, wu_ref, wd_ref = (refs.pop(0) for _ in range(4))
    if final:
        fw_ref = refs.pop(0)
    o_ref = refs.pop(0)

    x = x_ref[...]
    if proj:
        for i, y_ref in enumerate(y_refs):
            x = x + jnp.dot(y_ref[...].astype(bf16), wo_ref[i * W_GROUP:(i + 1) * W_GROUP, :],
                            preferred_element_type=f32)
    xn = _rms_norm(x, nw_ref[...]).astype(bf16)
    acc = None
    lo = 0
    for width in splits:
        hi = lo + width
        g = jnp.dot(xn, wg_ref[:, lo:hi], preferred_element_type=f32)
        u = jnp.dot(xn, wu_ref[:, lo:hi], preferred_element_type=f32)
        h = (jax.nn.silu(g) * u).astype(bf16)
        part = jnp.dot(h, wd_ref[lo:hi, :], preferred_element_type=f32)
        acc = part if acc is None else acc + part
        lo = hi
    out = x + 0.5 * acc
    if final:
        out = _rms_norm(out, fw_ref[...])
    o_ref[...] = out


def _ffn(x2, nw, wg, wu, wd, layer, *, tm, ys=None, w_out=None, final_w=None):
    t, d = x2.shape
    f = wg.shape[-1]
    proj = ys is not None
    final = final_w is not None
    tile = lambda width: pl.BlockSpec((tm, width), lambda i: (i, 0))
    row = pl.BlockSpec((1, d), lambda i: (0, 0))
    row_of_layer = pl.BlockSpec((None, 1, d), lambda i: (layer, 0, 0))
    resident = lambda r, c: pl.BlockSpec((None, r, c), lambda i: (layer, 0, 0), pipeline_mode=pl.Buffered(1))
    ins = [x2] + (list(ys) + [w_out] if proj else []) + [nw, wg, wu, wd] + ([final_w] if final else [])
    in_specs = [tile(d)] + ([tile(W_GROUP)] * 4 + [resident(d, d)] if proj else []) \
        + [row_of_layer, resident(d, f), resident(d, f), resident(f, d)] + ([row] if final else [])
    return pl.pallas_call(
        functools.partial(_ffn_body, splits=FFN_SPLITS, proj=proj, final=final),
        grid=(t // tm,),
        in_specs=in_specs,
        out_specs=tile(d),
        out_shape=jax.ShapeDtypeStruct((t, d), f32),
        compiler_params=pltpu.CompilerParams(
            dimension_semantics=("parallel",), vmem_limit_bytes=VMEM_LIMIT),
        name="ffn",
    )(*ins)


def _gla_stages(R, P, nb):
    L = GLA_CHUNK
    tc = MIX_TILE
    nch = tc // L
    row = lax.broadcasted_iota(jnp.int32, (L, 4 * L), 0)
    col = lax.broadcasted_iota(jnp.int32, (L, 4 * L), 1)
    causal = (col % L) <= row
    srow = lax.broadcasted_iota(jnp.int32, (W_GROUP, GLA_DK), 0) // GLA_HV
    scol = lax.broadcasted_iota(jnp.int32, (W_GROUP, GLA_DK), 1) // GLA_HK
    blockdiag = srow == scol
    k_masks = _head_masks(L, GLA_DK)
    v_masks = _head_masks(L, W_GROUP)
    pre, intra, kv, dec = {}, {}, {}, {}
    st = {}
    outs = [[None] * nch for _ in range(nb)]

    def prep(b):
        x = P[b]["gla"]
        z = _dot(P[b]["sm"], R["g_wup"][...]) + R["g_bias"][...]
        bc = _cumsum_rows(jax.nn.log_sigmoid(z) / GLA_GATE_NORM, L)
        q_dec = (x[:, 0:128] * (GLA_HK ** -0.5) * jnp.exp(bc)).astype(bf16)
        pre[b] = (x, bc, q_dec, x[:, 128:256] * jnp.exp(-bc))
        st[b] = R["g_st"][b]

    def chain(j, b):
        x, bc, q_dec, k_dec = pre[b]
        rows = slice(j * L, (j + 1) * L)
        s_cat = _dot_nt(q_dec[rows], _stack_heads(k_dec[rows], k_masks))
        s_cat = jnp.where(causal, s_cat, 0.0)
        v = x[rows, 256:512]
        intra[j, b] = _dot(s_cat, _stack_heads(v, v_masks))
        b_last = bc[(j + 1) * L - 1:(j + 1) * L, :]
        k_end = x[rows, 128:256] * jnp.exp(b_last - bc[rows])
        kv[j, b] = jnp.where(blockdiag, _dot_tn(v, k_end), 0.0)
        dec[j, b] = jnp.exp(b_last)

    def link(j, b):
        q_dec = pre[b][2]
        outs[b][j] = intra[j, b] + _dot_nt(q_dec[j * L:(j + 1) * L], st[b])
        st[b] = st[b] * dec[j, b] + kv[j, b]

    def finish(b):
        R["g_st"][b] = st[b]
        o = jnp.concatenate(outs[b], axis=0)
        ms = _seg_dot(o * o, R["ones"][...]) * (1.0 / GLA_HV)
        o = o * lax.rsqrt(ms + GROUP_NORM_EPS) * R["g_nw"][...]
        R["o_gla"][b] = o * jax.nn.silu(pre[b][0][:, 512:768])

    P_ = functools.partial
    chains = [(j, b) for j in range(nch) for b in range(nb)]
    return ([(350, P_(prep, b)) for b in range(nb)] + [(150, P_(chain, j, b)) for j, b in chains]
            + [(80, P_(link, j, b)) for j, b in chains] + [(300, P_(finish, b)) for b in range(nb)])


def _lru_stages(R, P, nb):
    tc = MIX_TILE
    w = W_GROUP
    row = lax.broadcasted_iota(jnp.int32, (tc, w), 0)
    au = {}

    def gates(b):
        x = P[b]["lru"]
        xb = x[:, 0:w]
        ext = R["l_ext"]
        ext[b, 8:, :] = xb
        xc = R["l_cb"][...] + R["l_cw"][CONV_W - 1:CONV_W, :] * xb
        for j in range(1, CONV_W):
            xc = xc + R["l_cw"][CONV_W - 1 - j:CONV_W - j, :] * ext[b, pl.ds(8 - j, tc), :]
        ext[b, 0:8, :] = xb[tc - 8:tc, :]
        ri = jax.nn.sigmoid(_dot(xc, R["l_wax"][...]) + R["l_bax"][...])
        log_a = -LRU_C * ri[:, 0:w] * jax.nn.softplus(-R["l_lam"][...])
        a = jnp.exp(log_a)
        au[b] = (a, jnp.sqrt(1.0 - a * a) * (ri[:, w:2 * w] * xc))

    def scan(b, shifts):
        a, u = au[b]
        for s in shifts:
            keep = row >= s
            u = u + a * jnp.where(keep, pltpu.roll(u, s, 0), 0.0)
            a = a * jnp.where(keep, pltpu.roll(a, s, 0), 1.0)
        au[b] = (a, u)

    def finish(b):
        a, u = au[b]
        h = u + a * R["l_h"][b]
        R["l_h"][b] = h[tc - 1:tc, :]
        R["o_lru"][b] = h * jax.nn.gelu(P[b]["lru"][:, w:2 * w])

    shifts = [1 << i for i in range(tc.bit_length() - 1)]
    P_ = functools.partial
    out = []
    for b in range(nb):
        out += [(700, P_(gates, b))] + [(260, P_(scan, b, shifts[i:i + 2])) for i in range(0, len(shifts), 2)] \
            + [(250, P_(finish, b))]
    return out


def _rw_stages(R, P, nb, mix):
    L = RW_CHUNK
    tc = MIX_TILE
    nch = tc // L
    w = W_GROUP
    n4 = 4 * L
    row3 = lax.broadcasted_iota(jnp.int32, (tc, 3 * w), 0)
    rows_sm = lax.broadcasted_iota(jnp.int32, (tc, SM_W), 0)
    rr = lax.broadcasted_iota(jnp.int32, (n4, n4), 0)
    cc = lax.broadcasted_iota(jnp.int32, (n4, n4), 1)
    strict = (rr % L) > (cc % L)
    incl = (rr % L) >= (cc % L)
    eye = (rr == cc).astype(f32).astype(bf16)
    masks = _head_masks(L, w)
    w0, a0, k_k, k_a, r_k, gn_w, gn_b, v0 = (R["r_vec"][i:i + 1, :] for i in range(8))
    lvl = R["lvl"]
    pre, hs = {}, {}
    ch = {}
    ys = [[None] * nch for _ in range(nb)]
    keys = [(j, b) for j in range(nch) for b in range(nb)]

    def prep_a(b):
        x = P[b]["rw"]
        sm = P[b]["sm"]
        x_prev = jnp.where(row3 == 0, R["r_cx"][b], pltpu.roll(x, 1, 0))
        sm_prev = jnp.where(rows_sm == 0, R["r_cs"][b], pltpu.roll(sm, 1, 0))
        R["r_cx"][b] = x[tc - 1:tc, :]
        R["r_cs"][b] = sm[tc - 1:tc, :]
        x = x + (x_prev - x) * R["r_mu"][...]
        sm = sm + (sm_prev - sm) * R["r_mus"][...]
        r = x[:, 0:w]
        k = x[:, w:2 * w]
        v = x[:, 2 * w:3 * w]
        log_w = -RW_DECAY_SCALE * jax.nn.sigmoid(w0 + _dot(jnp.tanh(sm), R["r_w2"][...]))
        a = jax.nn.sigmoid(a0 + _dot(sm, R["r_a2"][...]))
        g = _dot(jax.nn.sigmoid(sm), R["r_g2"][...])
        if mix:
            lam = jax.nn.sigmoid(v0 + _dot(_dot(v, R["r_v1"][...]), R["r_v2"][...]))
            v = v + (R["vf"][b] - v) * lam
        else:
            R["o_v"][b] = v
        pre[b] = dict(r=r, k=k, v=v, g=g, a=a, log_w=log_w)

    def prep_b(b):
        p = pre[b]
        r, k, v, g, a, log_w = (p[n] for n in ("r", "k", "v", "g", "a", "log_w"))
        kk = k * k_k
        n2 = _seg_dot(kk * kk, R["ones"][...])
        kk = kk / jnp.maximum(jnp.sqrt(n2), 1e-12)
        k = k * (1.0 + (a - 1.0) * k_a)
        cum = _cumsum_rows(log_w, L)
        g_in = jnp.exp(cum)
        g_inv = jnp.exp(-cum)
        pre[b] = dict(r=r, k=k, v=v, g=g, g_in=g_in,
                      al=-kk * jnp.exp(cum - log_w),
                      be=kk * a * g_inv, kt=k * g_inv, rt=r * g_in)
        hs[b] = R["r_h"][b]

    def scores(j, b):
        p = pre[b]
        rows = slice(j * L, (j + 1) * L)
        g_last = p["g_in"][(j + 1) * L - 1:(j + 1) * L, :]
        d = dict()
        d["g_col"] = jnp.broadcast_to(jnp.broadcast_to(g_last, (8, w)).T[:, 0:1], (w, w))
        d["al"] = _stack_heads(p["al"][rows], masks)
        d["rt"] = _stack_heads(p["rt"][rows], masks)
        d["vst"] = _stack_heads(p["v"][rows], masks)
        be = _stack_heads(p["be"][rows], masks)
        kt = _stack_heads(p["kt"][rows], masks)
        d["be_end"] = _stack_heads(p["be"][rows] * g_last, masks)
        kt_end = _stack_heads(p["kt"][rows] * g_last, masks)
        aa = _dot_nt(jnp.concatenate([d["al"], d["rt"]], axis=0), jnp.concatenate([be, kt], axis=0))
        d["a_ab"] = jnp.where(strict, aa[0:n4, 0:n4], 0.0).astype(bf16)
        d["a_ak"] = jnp.where(strict, aa[0:n4, n4:2 * n4], 0.0).astype(bf16)
        d["a_rb"] = jnp.where(incl, aa[n4:2 * n4, 0:n4], 0.0).astype(bf16)
        d["a_rk"] = jnp.where(incl, aa[n4:2 * n4, n4:2 * n4], 0.0).astype(bf16)
        d["kv"] = _dot_tn(kt_end, d["vst"])
        ch[j, b] = d

    def level0():
        for key in keys:
            ch[key]["t"] = eye + ch[key]["a_ab"] * lvl[0]

    def level_a(lv, part):
        for key in part:
            ch[key]["x1"] = _dot(ch[key]["t"], ch[key]["a_ab"] * lvl[lv])

    def level_b(part):
        for key in part:
            ch[key]["t"] = ch[key]["t"] + _dot(ch[key]["x1"], ch[key]["t"]).astype(bf16)

    def solve(j, b):
        d = ch[j, b]
        ak = _dot(jnp.concatenate([d["a_ak"], d["a_rk"]], axis=0), d["vst"])
        pq = _dot(d["t"], jnp.concatenate([d["al"], ak[0:n4].astype(bf16)], axis=1))
        d["p_st"] = pq[:, 0:w].astype(bf16)
        d["q_st"] = pq[:, w:2 * w]
        d["y_k"] = ak[n4:2 * n4]

    def link(j, b):
        d = ch[j, b]
        h_bf = hs[b].astype(bf16)
        u = (_dot(d["p_st"], h_bf) + d["q_st"]).astype(bf16)
        hs[b] = hs[b] * d["g_col"] + _dot_tn(d["be_end"], u) + d["kv"]
        y_st = _dot(d["rt"], h_bf) + _dot(d["a_rb"], u) + d["y_k"]
        ys[b][j] = y_st[0:L] + y_st[L:2 * L] + y_st[2 * L:3 * L] + y_st[3 * L:4 * L]

    def finish(b):
        R["r_h"][b] = hs[b]
        p = pre[b]
        y = jnp.concatenate(ys[b], axis=0)
        mean = _seg_dot(y, R["ones"][...]) * (1.0 / RW_HS)
        dlt = y - mean
        var = _seg_dot(dlt * dlt, R["ones"][...]) * (1.0 / RW_HS)
        yn = dlt * lax.rsqrt(var + RW_GN_EPS) * gn_w + gn_b
        bonus = _seg_dot(p["r"] * p["k"] * r_k, R["ones"][...]) * p["v"]
        R["o_rw"][b] = (yn + bonus) * p["g"]

    P_ = functools.partial
    n_lvl = RW_CHUNK.bit_length() - 1
    halves = (keys[:len(keys) // 2], keys[len(keys) // 2:])
    return ([st for b in range(nb) for st in ((900, P_(prep_a, b)), (900, P_(prep_b, b)))],
            [(450, P_(scores, j, b)) for j, b in keys] + [(150, level0)]
            + [(300, st) for lv in range(1, n_lvl) for st in
               (P_(level_a, lv, halves[0]), P_(level_a, lv, halves[1]), P_(level_b, halves[0]), P_(level_b, halves[1]))]
            + [(350, P_(solve, j, b)) for j, b in keys],
            [(300, P_(link, j, b)) for j, b in keys] + [(500, P_(finish, b)) for b in range(nb)])


def _ssd_stages(R, P, nb):
    L = SSD_CHUNK
    tc = MIX_TILE
    nch = tc // L
    w = W_GROUP
    cdim = 3 * w
    ll = lax.broadcasted_iota(jnp.int32, (L, L), 0)
    ss = lax.broadcasted_iota(jnp.int32, (L, L), 1)
    causal = ll >= ss
    lane_head = lax.broadcasted_iota(jnp.int32, (L, w), 1) // (w // SSD_HEADS)
    pre, hst = {}, {}
    ys = [[None] * nch for _ in range(nb)]

    def prep(b):
        x = P[b]["ssd"]
        xbc = x[:, w:4 * w]
        ext = R["s_ext"]
        ext[b, 8:, :] = xbc
        conv = R["s_cb"][...] + R["s_cw"][CONV_W - 1:CONV_W, :] * xbc
        for j in range(1, CONV_W):
            conv = conv + R["s_cw"][CONV_W - 1 - j:CONV_W - j, :] * ext[b, pl.ds(8 - j, tc), :]
        ext[b, 0:8, :] = xbc[tc - 8:tc, :]
        xbc = jax.nn.silu(conv)
        dt = jax.nn.softplus(P[b]["sm"] + R["s_dtb"][...])
        d_a = dt * (-jnp.exp(R["s_alog"][0:1, :]) * R["s_alog"][1:2, :])
        cs = _cumsum_rows(d_a, L)
        pre[b] = dict(xbc=xbc, cs=cs, dt_full=_expand_heads(dt, SM_DT, w), cs_full=_expand_heads(cs, SM_DT, w))
        hst[b] = R["s_h"][b]

    cur = {}

    def chunk_pre(j, b):
        p = pre[b]
        rows = slice(j * L, (j + 1) * L)
        xs = p["xbc"][rows, 0:w]
        cs = p["cs"][rows]
        cs_full = p["cs_full"][rows]
        cs_last = cs_full[L - 1:L, :]
        xdt = xs * p["dt_full"][rows]
        cur[b] = dict(xs=xs, bm=p["xbc"][rows, w:2 * w], cm=p["xbc"][rows, 2 * w:3 * w], cs=cs, cs_t=cs.T,
                      cs_full=cs_full, cs_last=cs_last, xdt=xdt, xw=xdt * jnp.exp(cs_last - cs_full),
                      y=jnp.zeros((L, w), f32), y_off=[], st=[])
        cur[b]["bm_t"] = cur[b]["bm"].T

    def group(b, grp):
        q = cur[b]
        lo, hi = grp * SSD_DSTATE, (grp + 1) * SSD_DSTATE
        cb_g = _dot_nt(q["cm"][:, lo:hi], q["bm"][:, lo:hi])
        for r in range(2):
            hd = grp * 2 + r
            seg = q["cs"][:, SM_DT + hd:SM_DT + hd + 1] - q["cs_t"][SM_DT + hd:SM_DT + hd + 1, :]
            m = cb_g * jnp.where(causal, jnp.exp(seg), 0.0)
            q["y"] = q["y"] + _dot(m, jnp.where(lane_head == hd, q["xdt"], 0.0))
        q["y_off"].append(_dot(q["cm"][:, lo:hi], hst[b][:, lo:hi]))
        q["st"].append(_dot(q["bm_t"][lo:hi, :], q["xw"][:, lo:hi]))

    def chunk_fin(j, b):
        q = cur[b]
        y = q["y"] + jnp.concatenate(q["y_off"], axis=1) * jnp.exp(q["cs_full"])
        hst[b] = hst[b] * jnp.exp(q["cs_last"]) + jnp.concatenate(q["st"], axis=1)
        ys[b][j] = y + R["s_dsk"][...] * q["xs"]

    def finish(b):
        R["s_h"][b] = hst[b]
        y = jnp.concatenate(ys[b], axis=0) * jax.nn.silu(P[b]["ssd"][:, 0:w])
        outs = []
        for grp in range(2):
            yg = y[:, grp * 128:(grp + 1) * 128]
            ms = jnp.mean(yg * yg, axis=-1, keepdims=True)
            outs.append(yg * lax.rsqrt(ms + GROUP_NORM_EPS))
        R["o_ssd"][b] = jnp.concatenate(outs, axis=1) * R["s_nw"][...]

    P_ = functools.partial
    out = []
    for b in range(nb):
        out += [(1000, P_(prep, b))]
        for j in range(nch):
            out += [(200, P_(chunk_pre, j, b)), (350, P_(group, b, 0)), (350, P_(group, b, 1)), (150, P_(chunk_fin, j, b))]
        out += [(350, P_(finish, b))]
    return out


_MIX_IN = ("x", "nw", "w_in", "g_wup", "g_bias", "g_nw", "l_cw", "l_cb", "l_wax", "l_bax", "l_lam",
           "r_mu", "r_mus", "r_w2", "r_a2", "r_g2", "r_vec", "s_cw", "s_cb", "s_dtb", "s_alog", "s_dsk", "s_nw",
           "ones", "lvl")
_MIX_SCRATCH = ("g_st", "l_ext", "l_h", "r_cx", "r_cs", "r_h", "s_ext", "s_h")


def _mix_body(*refs, names, nb, mix):
    R = dict(zip(names, refs))
    c = pl.program_id(0)

    @pl.when(c == 0)
    def _():
        for name in ("g_st", "l_h", "r_cx", "r_cs", "r_h", "s_h"):
            R[name][...] = jnp.zeros_like(R[name])
        for b in range(nb):
            R["l_ext"][b, 0:8, :] = jnp.zeros((8, W_GROUP), f32)
            R["s_ext"][b, 0:8, :] = jnp.zeros((8, 3 * W_GROUP), f32)

    P = [dict() for _ in range(nb)]
    xn = {}

    def norm(b):
        xn[b] = _rms_norm(R["x"][b], R["nw"][...]).astype(bf16)

    def project(b, seg):
        lo, hi = _SEGS[seg]
        P[b][seg] = jnp.dot(xn[b], R["w_in"][:, lo:hi], preferred_element_type=f32)

    P_ = functools.partial
    batches = range(nb)
    rw_prep, rw_chain, rw_tail = _rw_stages(R, P, nb, mix)
    head = [P_(norm, b) for b in batches] + [P_(project, b, seg) for b in batches for seg in ("rw", "sm")]
    rest_proj = [((_SEGS[seg][1] - _SEGS[seg][0]), P_(project, b, seg)) for seg in ("ssd", "lru", "gla") for b in batches]
    others = _ssd_stages(R, P, nb) + _lru_stages(R, P, nb) + _gla_stages(R, P, nb)
    for stage in head + _merge(rw_prep, rest_proj) + _merge(rw_chain + rw_tail, others):
        stage()


def _mixers(x3, nw, w_in_p, layer, v_first, prm):
    nb, s, d = x3.shape
    tc = MIX_TILE
    w = W_GROUP
    mix = v_first is not None
    blk = lambda width: pl.BlockSpec((nb, tc, width), lambda c: (0, c, 0))
    full = lambda a: pl.BlockSpec(a.shape, lambda c: (0,) * a.ndim)
    of_layer = lambda a: pl.BlockSpec((None,) + a.shape[1:], lambda c: (layer,) + (0,) * (a.ndim - 1))
    names = list(_MIX_IN) + (["r_v1", "r_v2", "vf"] if mix else [])
    arrays = dict(prm, x=x3, nw=nw, w_in=w_in_p, vf=v_first)
    specs = dict(x=blk(d), vf=blk(w), ones=full(prm["ones"]), lvl=full(prm["lvl"]),
                 w_in=pl.BlockSpec((None, d, w_in_p.shape[-1]), lambda c: (layer, 0, 0),
                                   pipeline_mode=pl.Buffered(1)))
    ins = [arrays[n] for n in names]
    in_specs = [specs[n] if n in specs else of_layer(arrays[n]) for n in names]
    out_names = ["o_gla", "o_lru", "o_rw", "o_ssd"] + ([] if mix else ["o_v"])
    out_shape = jax.ShapeDtypeStruct((nb, s, w), f32)
    scratch = [pltpu.VMEM((nb, w, GLA_DK), f32), pltpu.VMEM((nb, tc + 8, w), f32), pltpu.VMEM((nb, 1, w), f32),
               pltpu.VMEM((nb, 1, 3 * w), f32), pltpu.VMEM((nb, 1, SM_W), f32), pltpu.VMEM((nb, w, w), f32),
               pltpu.VMEM((nb, tc + 8, 3 * w), f32), pltpu.VMEM((nb, SSD_DSTATE, w), f32)]
    outs = pl.pallas_call(
        functools.partial(_mix_body, names=tuple(names + out_names) + _MIX_SCRATCH, nb=nb, mix=mix),
        grid=(s // tc,),
        in_specs=in_specs,
        out_specs=[blk(w)] * len(out_names),
        out_shape=[out_shape] * len(out_names),
        scratch_shapes=scratch,
        compiler_params=pltpu.CompilerParams(
            dimension_semantics=("arbitrary",), vmem_limit_bytes=VMEM_LIMIT),
        name="mixers",
    )(*ins)
    return outs[:4], (v_first if mix else outs[4])


def _level_masks():
    n = 4 * RW_CHUNK
    t = np.arange(n)[:, None]
    u = np.arange(n)[None, :]
    out = []
    s = 1
    while s < RW_CHUNK:
        out.append(((t // (2 * s)) == (u // (2 * s))) & ((t % (2 * s)) >= s) & ((u % (2 * s)) < s))
        s *= 2
    return jnp.asarray(np.stack(out).astype(np.float32), dtype=bf16)


def _ones_blocks(seg):
    i = np.arange(W_GROUP)
    return jnp.asarray((i[:, None] // seg == i[None, :] // seg).astype(np.float32), dtype=bf16)


def _rows(a):
    return a[:, None, :]


def _pad_rows(m, row0):
    return jnp.pad(m, ((0, 0), (row0, SM_W - row0 - m.shape[1]), (0, 0)))


def _lane_rows(v, lane0):
    return jnp.pad(v, ((0, 0), (lane0, SM_W - lane0 - v.shape[1])))[:, None, :]


def _block_diag(w):
    nl, n, k, j = w.shape
    return jnp.einsum("lnkj,nm->lnkmj", w, jnp.eye(n, dtype=w.dtype)).reshape(nl, n * k, n * j)


def kernel(x, ffn1_norm, ffn1_w_gate, ffn1_w_up, ffn1_w_down, mix_norm, w_in, w_out, gla_alpha_up, gla_alpha_bias, gla_norm, lru_conv_w, lru_conv_b, lru_w_a, lru_b_a, lru_w_x, lru_b_x, lru_lambda, rw_mu, rw_w0, rw_w2, rw_a0, rw_a2, rw_g2, rw_v0, rw_v1, rw_v2, rw_k_k, rw_k_a, rw_r_k, rw_gn_w, rw_gn_b, ssd_conv_w, ssd_conv_b, ssd_dt_bias, ssd_a_log, ssd_d, ssd_norm, ffn2_norm, ffn2_w_gate, ffn2_w_up, ffn2_w_down, final_norm):
    nb, s, d = x.shape
    t = nb * s
    depth = w_in.shape[0]
    tm = min(512, t)

    wg1, wu1, wd1 = (a.astype(bf16) for a in (ffn1_w_gate, ffn1_w_up, ffn1_w_down))
    wg2, wu2, wd2 = (a.astype(bf16) for a in (ffn2_w_gate, ffn2_w_up, ffn2_w_down))
    w_out_b = w_out.astype(bf16)
    w_in_b = w_in.astype(bf16)
    w_in_p = jnp.concatenate([
        w_in_b[:, :, _GLA0:_GLA0 + 768], w_in_b[:, :, _LRU0:_LRU0 + 512], w_in_b[:, :, _RW0:_RW0 + 768],
        w_in_b[:, :, _SSD0:_SSD0 + 1024],
        w_in_b[:, :, _GLA0 + 768:_GLA0 + 784], w_in_b[:, :, _RW0 + 768:_RW0 + 832],
        w_in_b[:, :, _SSD0 + 1024:_SSD0 + 1028], jnp.zeros((depth, d, SM_W - 84), bf16)], axis=-1)
    rank_v = rw_v1.shape[-1]
    prm = dict(
        g_wup=_pad_rows(gla_alpha_up, SM_GLA), g_bias=_rows(gla_alpha_bias),
        g_nw=_rows(jnp.tile(gla_norm, (1, GLA_HEADS))),
        l_cw=lru_conv_w, l_cb=_rows(lru_conv_b),
        l_wax=jnp.concatenate([_block_diag(lru_w_a), _block_diag(lru_w_x)], axis=-1).astype(bf16),
        l_bax=_rows(jnp.concatenate([lru_b_a, lru_b_x], axis=-1)), l_lam=_rows(lru_lambda),
        r_mu=_rows(rw_mu[:, 0:768]), r_mus=_lane_rows(rw_mu[:, 768:832], SM_RW),
        r_w2=_pad_rows(rw_w2, SM_RW), r_a2=_pad_rows(rw_a2, SM_RW + 16), r_g2=_pad_rows(rw_g2, SM_RW + 32),
        r_vec=jnp.stack([rw_w0, rw_a0, rw_k_k, rw_k_a, rw_r_k.reshape(depth, -1), rw_gn_w, rw_gn_b,
                         jnp.pad(rw_v0, ((1, 0), (0, 0)))], axis=1),
        r_v1=jnp.pad(rw_v1, ((1, 0), (0, 0), (0, 128 - rank_v))),
        r_v2=jnp.pad(rw_v2, ((1, 0), (0, 128 - rank_v), (0, 0))),
        s_cw=ssd_conv_w, s_cb=_rows(ssd_conv_b), s_dtb=_lane_rows(ssd_dt_bias, SM_DT),
        s_alog=jnp.concatenate([_lane_rows(ssd_a_log, SM_DT), _lane_rows(jnp.ones_like(ssd_a_log), SM_DT)], axis=1),
        s_dsk=_rows(jnp.repeat(ssd_d, W_GROUP // SSD_HEADS, axis=-1)), s_nw=_rows(ssd_norm),
        ones=_ones_blocks(64), lvl=_level_masks())
    n1, n2, nm = _rows(ffn1_norm), _rows(ffn2_norm), _rows(mix_norm)

    x2 = x.reshape(t, d)
    v_first = None
    for l in range(depth):
        x2 = _ffn(x2, n1, wg1, wu1, wd1, l, tm=tm)
        ys, v_first = _mixers(x2.reshape(nb, s, d), nm, w_in_p, l, v_first, prm)
        ys = [a.reshape(t, W_GROUP) for a in ys]
        x2 = _ffn(x2, n2, wg2, wu2, wd2, l, tm=tm, ys=ys, w_out=w_out_b,
                  final_w=final_norm[None, :] if l == depth - 1 else None)
    return x2.reshape(nb, s, d)
```

```python
import functools
import math

import numpy as np
import jax
import jax.numpy as jnp
from jax import lax
from jax.experimental import pallas as pl
from jax.experimental.pallas import tpu as pltpu

f32 = jnp.float32
bf16 = jnp.bfloat16

D_MODEL = 1024
D_FF = 2816
W_GROUP = 256
NORM_EPS = 1e-6
GROUP_NORM_EPS = 1e-5
CONV_W = 4

GLA_HEADS = 4
GLA_DK = 128
GLA_HK = 32
GLA_HV = 64
GLA_GATE_NORM = 16.0
GLA_CHUNK = 64

LRU_C = 8.0

RW_HS = 64
RW_DECAY_SCALE = math.exp(-0.5)
RW_GN_EPS = 64e-5
RW_CHUNK = 64

SSD_HEADS = 4
SSD_DSTATE = 128
SSD_CHUNK = 128

_GLA0, _LRU0, _RW0, _SSD0 = 0, 784, 1296, 2128
_SEGS = dict(gla=(0, 768), lru=(768, 1280), rw=(1280, 2048), ssd=(2048, 3072), sm=(3072, 3200))
SM_GLA = 0
SM_RW = 16
SM_DT = 80
SM_W = 128

MIX_TILE = 256
FFN_TILE = 1024
MXU_TILE = 256
FFN_SPLITS = (MXU_TILE,) * (D_FF // MXU_TILE)
VMEM_LIMIT = 56 * 1024 * 1024


def _dot(a, b):
    return jnp.dot(a.astype(bf16), b.astype(bf16), preferred_element_type=f32)


def _dot_nt(a, b):
    return lax.dot_general(a.astype(bf16), b.astype(bf16), (((1,), (1,)), ((), ())),
                           preferred_element_type=f32)


def _dot_tn(a, b):
    return lax.dot_general(a.astype(bf16), b.astype(bf16), (((0,), (0,)), ((), ())),
                           preferred_element_type=f32)


def _seg_dot(x, ones_bf):
    return jnp.dot(x.astype(bf16), ones_bf, preferred_element_type=f32)


def _cumsum_rows(x, seg):
    row = lax.broadcasted_iota(jnp.int32, x.shape, 0) % seg
    s = 1
    while s < seg:
        x = x + jnp.where(row >= s, pltpu.roll(x, s, 0), 0.0)
        s *= 2
    return x


def _head_masks(rows, width):
    lane = lax.broadcasted_iota(jnp.int32, (rows, width), 1) // (width // 4)
    return [(lane == h).astype(f32).astype(bf16) for h in range(4)]


def _stack_heads(x, masks):
    xb = x.astype(bf16)
    return jnp.concatenate([xb * m for m in masks], axis=0)


def _expand_heads(x, lane0, width):
    n = x.shape[0]
    lane = lax.broadcasted_iota(jnp.int32, (n, width), 1) // (width // 4)
    out = jnp.zeros((n, width), f32)
    for h in range(4):
        out = jnp.where(lane == h, x[:, lane0 + h:lane0 + h + 1], out)
    return out


def _rms_norm(x, w):
    ms = jnp.mean(x * x, axis=-1, keepdims=True)
    return x * lax.rsqrt(ms + NORM_EPS) * w


def _merge(*stage_lists):
    keyed = []
    for k, stages in enumerate(stage_lists):
        total = float(sum(c for c, _ in stages))
        done = 0.0
        for i, (c, st) in enumerate(stages):
            keyed.append(((done + 0.5 * c) / total, k, i, st))
            done += c
    return [st for _, _, _, st in sorted(keyed, key=lambda t: t[:3])]


def _ffn_body(*refs, splits, proj, final):
    refs = list(refs)
    x_ref = refs.pop(0)
    if proj:
        y_refs = [refs.pop(0) for _ in range(4)]
        wo_ref = refs.pop(0)
    nw_ref, wg_ref, wu_ref, wd_ref = (refs.pop(0) for _ in range(4))
    if final:
        fw_ref = refs.pop(0)
    o_ref = refs.pop(0)

    x = x_ref[...]
    if proj:
        for i, y_ref in enumerate(y_refs):
            x = x + jnp.dot(y_ref[...], wo_ref[i * W_GROUP:(i + 1) * W_GROUP, :],
                            preferred_element_type=f32)
    xn = _rms_norm(x, nw_ref[...]).astype(bf16)
    acc = None
    lo = 0
    for width in splits:
        hi = lo + width
        g = jnp.dot(xn, wg_ref[:, lo:hi], preferred_element_type=f32)
        u = jnp.dot(xn, wu_ref[:, lo:hi], preferred_element_type=f32)
        h = (jax.nn.silu(g) * u).astype(bf16)
        part = jnp.dot(h, wd_ref[lo:hi, :], preferred_element_type=f32)
        acc = part if acc is None else acc + part
        lo = hi
    out = x + 0.5 * acc
    if final:
        out = _rms_norm(out, fw_ref[...])
    o_ref[...] = out


def _ffn(x2, nw, wg, wu, wd, layer, *, tm, ys=None, w_out=None, final_w=None):
    t, d = x2.shape
    f = wg.shape[-1]
    proj = ys is not None
    final = final_w is not None
    tile = lambda width: pl.BlockSpec((tm, width), lambda i: (i, 0))
    row = pl.BlockSpec((1, d), lambda i: (0, 0))
    row_of_layer = pl.BlockSpec((None, 1, d), lambda i: (layer, 0, 0))
    resident = lambda r, c: pl.BlockSpec((None, r, c), lambda i: (layer, 0, 0), pipeline_mode=pl.Buffered(1))
    ins = [x2] + (list(ys) + [w_out] if proj else []) + [nw, wg, wu, wd] + ([final_w] if final else [])
    in_specs = [tile(d)] + ([tile(W_GROUP)] * 4 + [resident(d, d)] if proj else []) \
        + [row_of_layer, resident(d, f), resident(d, f), resident(f, d)] + ([row] if final else [])
    return pl.pallas_call(
        functools.partial(_ffn_body, splits=FFN_SPLITS, proj=proj, final=final),
        grid=(t // tm,),
        in_specs=in_specs,
        out_specs=tile(d),
        out_shape=jax.ShapeDtypeStruct((t, d), f32),
        compiler_params=pltpu.CompilerParams(
            dimension_semantics=("parallel",), vmem_limit_bytes=VMEM_LIMIT),
        name="ffn",
    )(*ins)


def _gla_stages(R, P, nb):
    L = GLA_CHUNK
    tc = MIX_TILE
    nch = tc // L
    row = lax.broadcasted_iota(jnp.int32, (L, 4 * L), 0)
    col = lax.broadcasted_iota(jnp.int32, (L, 4 * L), 1)
    causal = (col % L) <= row
    srow = lax.broadcasted_iota(jnp.int32, (W_GROUP, GLA_DK), 0) // GLA_HV
    scol = lax.broadcasted_iota(jnp.int32, (W_GROUP, GLA_DK), 1) // GLA_HK
    blockdiag = srow == scol
    k_masks = _head_masks(L, GLA_DK)
    v_masks = _head_masks(L, W_GROUP)
    pre, intra, kv, dec = {}, {}, {}, {}
    st = {}
    outs = [[None] * nch for _ in range(nb)]

    def prep(b):
        x = P[b]["gla"]
        z = _dot(P[b]["sm"], R["g_wup"][...]) + R["g_bias"][...]
        bc = _cumsum_rows(jax.nn.log_sigmoid(z) / GLA_GATE_NORM, L)
        q_dec = (x[:, 0:128] * (GLA_HK ** -0.5) * jnp.exp(bc)).astype(bf16)
        pre[b] = (x, bc, q_dec, x[:, 128:256] * jnp.exp(-bc))
        st[b] = R["g_st"][b]

    def chain(j, b):
        x, bc, q_dec, k_dec = pre[b]
        rows = slice(j * L, (j + 1) * L)
        s_cat = _dot_nt(q_dec[rows], _stack_heads(k_dec[rows], k_masks))
        s_cat = jnp.where(causal, s_cat, 0.0)
        v = x[rows, 256:512]
        intra[j, b] = _dot(s_cat, _stack_heads(v, v_masks))
        b_last = bc[(j + 1) * L - 1:(j + 1) * L, :]
        k_end = x[rows, 128:256] * jnp.exp(b_last - bc[rows])
        kv[j, b] = jnp.where(blockdiag, _dot_tn(v, k_end), 0.0)
        dec[j, b] = jnp.exp(b_last)

    def link(j, b):
        q_dec = pre[b][2]
        outs[b][j] = intra[j, b] + _dot_nt(q_dec[j * L:(j + 1) * L], st[b])
        st[b] = st[b] * dec[j, b] + kv[j, b]

    def finish(b):
        R["g_st"][b] = st[b]
        o = jnp.concatenate(outs[b], axis=0)
        ms = _seg_dot(o * o, R["ones"][...]) * (1.0 / GLA_HV)
        o = o * lax.rsqrt(ms + GROUP_NORM_EPS) * R["g_nw"][...]
        R["o_gla"][b] = (o * jax.nn.silu(pre[b][0][:, 512:768])).astype(bf16)

    P_ = functools.partial
    chains = [(j, b) for j in range(nch) for b in range(nb)]
    return ([(350, P_(prep, b)) for b in range(nb)] + [(150, P_(chain, j, b)) for j, b in chains]
            + [(80, P_(link, j, b)) for j, b in chains] + [(300, P_(finish, b)) for b in range(nb)])


def _lru_stages(R, P, nb):
    tc = MIX_TILE
    w = W_GROUP
    row = lax.broadcasted_iota(jnp.int32, (tc, w), 0)
    au = {}

    def gates(b):
        x = P[b]["lru"]
        xb = x[:, 0:w]
        ext = R["l_ext"]
        ext[b, 8:, :] = xb
        xc = R["l_cb"][...] + R["l_cw"][CONV_W - 1:CONV_W, :] * xb
        for j in range(1, CONV_W):
            xc = xc + R["l_cw"][CONV_W - 1 - j:CONV_W - j, :] * ext[b, pl.ds(8 - j, tc), :]
        ext[b, 0:8, :] = xb[tc - 8:tc, :]
        ri = jax.nn.sigmoid(_dot(xc, R["l_wax"][...]) + R["l_bax"][...])
        log_a = -LRU_C * ri[:, 0:w] * jax.nn.softplus(-R["l_lam"][...])
        a = jnp.exp(log_a)
        au[b] = (a, jnp.sqrt(1.0 - a * a) * (ri[:, w:2 * w] * xc))

    def scan(b, shifts):
        a, u = au[b]
        for s in shifts:
            keep = row >= s
            u = u + a * jnp.where(keep, pltpu.roll(u, s, 0), 0.0)
            a = a * jnp.where(keep, pltpu.roll(a, s, 0), 1.0)
        au[b] = (a, u)

    def finish(b):
        a, u = au[b]
        h = u + a * R["l_h"][b]
        R["l_h"][b] = h[tc - 1:tc, :]
        R["o_lru"][b] = (h * jax.nn.gelu(P[b]["lru"][:, w:2 * w])).astype(bf16)

    shifts = [1 << i for i in range(tc.bit_length() - 1)]
    P_ = functools.partial
    out = []
    for b in range(nb):
        out += [(700, P_(gates, b))] + [(260, P_(scan, b, shifts[i:i + 2])) for i in range(0, len(shifts), 2)] \
            + [(250, P_(finish, b))]
    return out


def _rw_stages(R, P, nb, mix):
    L = RW_CHUNK
    tc = MIX_TILE
    nch = tc // L
    w = W_GROUP
    n4 = 4 * L
    row3 = lax.broadcasted_iota(jnp.int32, (tc, 3 * w), 0)
    rows_sm = lax.broadcasted_iota(jnp.int32, (tc, SM_W), 0)
    rr = lax.broadcasted_iota(jnp.int32, (n4, n4), 0)
    cc = lax.broadcasted_iota(jnp.int32, (n4, n4), 1)
    strict = (rr % L) > (cc % L)
    incl = (rr % L) >= (cc % L)
    eye = (rr == cc).astype(f32).astype(bf16)
    masks = _head_masks(L, w)
    w0, a0, k_k, k_a, r_k, gn_w, gn_b, v0 = (R["r_vec"][i:i + 1, :] for i in range(8))
    lvl = R["lvl"]
    pre, hs = {}, {}
    ch = {}
    ys = [[None] * nch for _ in range(nb)]
    keys = [(j, b) for j in range(nch) for b in range(nb)]

    def prep_a(b):
        x = P[b]["rw"]
        sm = P[b]["sm"]
        x_prev = jnp.where(row3 == 0, R["r_cx"][b], pltpu.roll(x, 1, 0))
        sm_prev = jnp.where(rows_sm == 0, R["r_cs"][b], pltpu.roll(sm, 1, 0))
        R["r_cx"][b] = x[tc - 1:tc, :]
        R["r_cs"][b] = sm[tc - 1:tc, :]
        x = x + (x_prev - x) * R["r_mu"][...]
        sm = sm + (sm_prev - sm) * R["r_mus"][...]
        r = x[:, 0:w]
        k = x[:, w:2 * w]
        v = x[:, 2 * w:3 * w]
        log_w = -RW_DECAY_SCALE * jax.nn.sigmoid(w0 + _dot(jnp.tanh(sm), R["r_w2"][...]))
        a = jax.nn.sigmoid(a0 + _dot(sm, R["r_a2"][...]))
        g = _dot(jax.nn.sigmoid(sm), R["r_g2"][...])
        if mix:
            lam = jax.nn.sigmoid(v0 + _dot(_dot(v, R["r_v1"][...]), R["r_v2"][...]))
            v = v + (R["vf"][b] - v) * lam
        else:
            R["o_v"][b] = v
        pre[b] = dict(r=r, k=k, v=v, g=g, a=a, log_w=log_w)

    def prep_b(b):
        p = pre[b]
        r, k, v, g, a, log_w = (p[n] for n in ("r", "k", "v", "g", "a", "log_w"))
        kk = k * k_k
        n2 = _seg_dot(kk * kk, R["ones"][...])
        kk = kk / jnp.maximum(jnp.sqrt(n2), 1e-12)
        k = k * (1.0 + (a - 1.0) * k_a)
        cum = _cumsum_rows(log_w, L)
        g_in = jnp.exp(cum)
        g_inv = jnp.exp(-cum)
        pre[b] = dict(r=r, k=k, v=v, g=g, g_in=g_in,
                      al=-kk * jnp.exp(cum - log_w),
                      be=kk * a * g_inv, kt=k * g_inv, rt=r * g_in)
        hs[b] = R["r_h"][b]

    def scores(j, b):
        p = pre[b]
        rows = slice(j * L, (j + 1) * L)
        g_last = p["g_in"][(j + 1) * L - 1:(j + 1) * L, :]
        d = dict()
        d["g_col"] = jnp.broadcast_to(jnp.broadcast_to(g_last, (8, w)).T[:, 0:1], (w, w))
        d["al"] = _stack_heads(p["al"][rows], masks)
        d["rt"] = _stack_heads(p["rt"][rows], masks)
        d["vst"] = _stack_heads(p["v"][rows], masks)
        be = _stack_heads(p["be"][rows], masks)
        kt = _stack_heads(p["kt"][rows], masks)
        d["be_end"] = _stack_heads(p["be"][rows] * g_last, masks)
        kt_end = _stack_heads(p["kt"][rows] * g_last, masks)
        aa = _dot_nt(jnp.concatenate([d["al"], d["rt"]], axis=0), jnp.concatenate([be, kt], axis=0))
        d["a_ab"] = jnp.where(strict, aa[0:n4, 0:n4], 0.0).astype(bf16)
        d["a_ak"] = jnp.where(strict, aa[0:n4, n4:2 * n4], 0.0).astype(bf16)
        d["a_rb"] = jnp.where(incl, aa[n4:2 * n4, 0:n4], 0.0).astype(bf16)
        d["a_rk"] = jnp.where(incl, aa[n4:2 * n4, n4:2 * n4], 0.0).astype(bf16)
        d["kv"] = _dot_tn(kt_end, d["vst"])
        ch[j, b] = d

    def level0():
        for key in keys:
            ch[key]["t"] = eye + ch[key]["a_ab"] * lvl[0]

    def level_a(lv, part):
        for key in part:
            ch[key]["x1"] = _dot(ch[key]["t"], ch[key]["a_ab"] * lvl[lv])

    def level_b(part):
        for key in part:
            ch[key]["t"] = ch[key]["t"] + _dot(ch[key]["x1"], ch[key]["t"]).astype(bf16)

    def solve(j, b):
        d = ch[j, b]
        ak = _dot(jnp.concatenate([d["a_ak"], d["a_rk"]], axis=0), d["vst"])
        pq = _dot(d["t"], jnp.concatenate([d["al"], ak[0:n4].astype(bf16)], axis=1))
        d["p_st"] = pq[:, 0:w].astype(bf16)
        d["q_st"] = pq[:, w:2 * w]
        d["y_k"] = ak[n4:2 * n4]

    def link(j, b):
        d = ch[j, b]
        h_bf = hs[b].astype(bf16)
        u = (_dot(d["p_st"], h_bf) + d["q_st"]).astype(bf16)
        hs[b] = hs[b] * d["g_col"] + _dot_tn(d["be_end"], u) + d["kv"]
        y_st = _dot(d["rt"], h_bf) + _dot(d["a_rb"], u) + d["y_k"]
        ys[b][j] = y_st[0:L] + y_st[L:2 * L] + y_st[2 * L:3 * L] + y_st[3 * L:4 * L]

    def finish(b):
        R["r_h"][b] = hs[b]
        p = pre[b]
        y = jnp.concatenate(ys[b], axis=0)
        mean = _seg_dot(y, R["ones"][...]) * (1.0 / RW_HS)
        dlt = y - mean
        var = _seg_dot(dlt * dlt, R["ones"][...]) * (1.0 / RW_HS)
        yn = dlt * lax.rsqrt(var + RW_GN_EPS) * gn_w + gn_b
        bonus = _seg_dot(p["r"] * p["k"] * r_k, R["ones"][...]) * p["v"]
        R["o_rw"][b] = ((yn + bonus) * p["g"]).astype(bf16)

    P_ = functools.partial
    n_lvl = RW_CHUNK.bit_length() - 1
    halves = (keys[:len(keys) // 2], keys[len(keys) // 2:])
    return ([st for b in range(nb) for st in ((900, P_(prep_a, b)), (900, P_(prep_b, b)))],
            [(450, P_(scores, j, b)) for j, b in keys] + [(150, level0)]
            + [(300, st) for lv in range(1, n_lvl) for st in
               (P_(level_a, lv, halves[0]), P_(level_a, lv, halves[1]), P_(level_b, halves[0]), P_(level_b, halves[1]))]
            + [(350, P_(solve, j, b)) for j, b in keys],
            [(300, P_(link, j, b)) for j, b in keys] + [(500, P_(finish, b)) for b in range(nb)])


def _ssd_stages(R, P, nb):
    L = SSD_CHUNK
    tc = MIX_TILE
    nch = tc // L
    w = W_GROUP
    cdim = 3 * w
    ll = lax.broadcasted_iota(jnp.int32, (L, L), 0)
    ss = lax.broadcasted_iota(jnp.int32, (L, L), 1)
    causal = ll >= ss
    lane_head = lax.broadcasted_iota(jnp.int32, (L, w), 1) // (w // SSD_HEADS)
    pre, hst = {}, {}
    ys = [[None] * nch for _ in range(nb)]

    def prep(b):
        x = P[b]["ssd"]
        xbc = x[:, w:4 * w]
        ext = R["s_ext"]
        ext[b, 8:, :] = xbc
        conv = R["s_cb"][...] + R["s_cw"][CONV_W - 1:CONV_W, :] * xbc
        for j in range(1, CONV_W):
            conv = conv + R["s_cw"][CONV_W - 1 - j:CONV_W - j, :] * ext[b, pl.ds(8 - j, tc), :]
        ext[b, 0:8, :] = xbc[tc - 8:tc, :]
        xbc = jax.nn.silu(conv)
        dt = jax.nn.softplus(P[b]["sm"] + R["s_dtb"][...])
        d_a = dt * (-jnp.exp(R["s_alog"][0:1, :]) * R["s_alog"][1:2, :])
        cs = _cumsum_rows(d_a, L)
        pre[b] = dict(xbc=xbc, cs=cs, dt_full=_expand_heads(dt, SM_DT, w), cs_full=_expand_heads(cs, SM_DT, w))
        hst[b] = R["s_h"][b]

    cur = {}

    def chunk_pre(j, b):
        p = pre[b]
        rows = slice(j * L, (j + 1) * L)
        xs = p["xbc"][rows, 0:w]
        cs = p["cs"][rows]
        cs_full = p["cs_full"][rows]
        cs_last = cs_full[L - 1:L, :]
        xdt = xs * p["dt_full"][rows]
        cur[b] = dict(xs=xs, bm=p["xbc"][rows, w:2 * w], cm=p["xbc"][rows, 2 * w:3 * w], cs=cs, cs_t=cs.T,
                      cs_full=cs_full, cs_last=cs_last, xdt=xdt, xw=xdt * jnp.exp(cs_last - cs_full),
                      y=jnp.zeros((L, w), f32), y_off=[], st=[])
        cur[b]["bm_t"] = cur[b]["bm"].T

    def group(b, grp):
        q = cur[b]
        lo, hi = grp * SSD_DSTATE, (grp + 1) * SSD_DSTATE
        cb_g = _dot_nt(q["cm"][:, lo:hi], q["bm"][:, lo:hi])
        for r in range(2):
            hd = grp * 2 + r
            seg = q["cs"][:, SM_DT + hd:SM_DT + hd + 1] - q["cs_t"][SM_DT + hd:SM_DT + hd + 1, :]
            m = cb_g * jnp.where(causal, jnp.exp(seg), 0.0)
            q["y"] = q["y"] + _dot(m, jnp.where(lane_head == hd, q["xdt"], 0.0))
        q["y_off"].append(_dot(q["cm"][:, lo:hi], hst[b][:, lo:hi]))
        q["st"].append(_dot(q["bm_t"][lo:hi, :], q["xw"][:, lo:hi]))

    def chunk_fin(j, b):
        q = cur[b]
        y = q["y"] + jnp.concatenate(q["y_off"], axis=1) * jnp.exp(q["cs_full"])
        hst[b] = hst[b] * jnp.exp(q["cs_last"]) + jnp.concatenate(q["st"], axis=1)
        ys[b][j] = y + R["s_dsk"][...] * q["xs"]

    def finish(b):
        R["s_h"][b] = hst[b]
        y = jnp.concatenate(ys[b], axis=0) * jax.nn.silu(P[b]["ssd"][:, 0:w])
        outs = []
        for grp in range(2):
            yg = y[:, grp * 128:(grp + 1) * 128]
            ms = jnp.mean(yg * yg, axis=-1, keepdims=True)
            outs.append(yg * lax.rsqrt(ms + GROUP_NORM_EPS))
        R["o_ssd"][b] = (jnp.concatenate(outs, axis=1) * R["s_nw"][...]).astype(bf16)

    P_ = functools.partial
    out = []
    for b in range(nb):
        out += [(1000, P_(prep, b))]
        for j in range(nch):
            out += [(200, P_(chunk_pre, j, b)), (350, P_(group, b, 0)), (350, P_(group, b, 1)), (150, P_(chunk_fin, j, b))]
        out += [(350, P_(finish, b))]
    return out


_MIX_IN = ("x", "nw", "w_in", "g_wup", "g_bias", "g_nw", "l_cw", "l_cb", "l_wax", "l_bax", "l_lam",
           "r_mu", "r_mus", "r_w2", "r_a2", "r_g2", "r_vec", "s_cw", "s_cb", "s_dtb", "s_alog", "s_dsk", "s_nw",
           "ones", "lvl")
_MIX_SCRATCH = ("g_st", "l_ext", "l_h", "r_cx", "r_cs", "r_h", "s_ext", "s_h")


def _mix_body(*refs, names, nb, mix):
    R = dict(zip(names, refs))
    c = pl.program_id(0)

    @pl.when(c == 0)
    def _():
        for name in ("g_st", "l_h", "r_cx", "r_cs", "r_h", "s_h"):
            R[name][...] = jnp.zeros_like(R[name])
        for b in range(nb):
            R["l_ext"][b, 0:8, :] = jnp.zeros((8, W_GROUP), f32)
            R["s_ext"][b, 0:8, :] = jnp.zeros((8, 3 * W_GROUP), f32)

    P = [dict() for _ in range(nb)]
    xn = {}

    def norm(b):
        xn[b] = _rms_norm(R["x"][b], R["nw"][...]).astype(bf16)

    def project(b, seg):
        lo, hi = _SEGS[seg]
        P[b][seg] = jnp.dot(xn[b], R["w_in"][:, lo:hi], preferred_element_type=f32)

    P_ = functools.partial
    batches = range(nb)
    rw_prep, rw_chain, rw_tail = _rw_stages(R, P, nb, mix)
    head = [P_(norm, b) for b in batches] + [P_(project, b, seg) for b in batches for seg in ("rw", "sm")]
    rest_proj = [((_SEGS[seg][1] - _SEGS[seg][0]), P_(project, b, seg)) for seg in ("ssd", "lru", "gla") for b in batches]
    others = _ssd_stages(R, P, nb) + _lru_stages(R, P, nb) + _gla_stages(R, P, nb)
    for stage in head + _merge(rw_prep, rest_proj) + _merge(rw_chain + rw_tail, others):
        stage()


def _mixers(x3, nw, w_in_p, layer, v_first, prm):
    nb, s, d = x3.shape
    tc = MIX_TILE
    w = W_GROUP
    mix = v_first is not None
    blk = lambda width: pl.BlockSpec((nb, tc, width), lambda c: (0, c, 0))
    full = lambda a: pl.BlockSpec(a.shape, lambda c: (0,) * a.ndim)
    of_layer = lambda a: pl.BlockSpec((None,) + a.shape[1:], lambda c: (layer,) + (0,) * (a.ndim - 1))
    names = list(_MIX_IN) + (["r_v1", "r_v2", "vf"] if mix else [])
    arrays = dict(prm, x=x3, nw=nw, w_in=w_in_p, vf=v_first)
    specs = dict(x=blk(d), vf=blk(w), ones=full(prm["ones"]), lvl=full(prm["lvl"]),
                 w_in=pl.BlockSpec((None, d, w_in_p.shape[-1]), lambda c: (layer, 0, 0),
                                   pipeline_mode=pl.Buffered(1)))
    ins = [arrays[n] for n in names]
    in_specs = [specs[n] if n in specs else of_layer(arrays[n]) for n in names]
    out_names = ["o_gla", "o_lru", "o_rw", "o_ssd"] + ([] if mix else ["o_v"])
    out_shape = [jax.ShapeDtypeStruct((nb, s, w), bf16)] * 4 + ([] if mix else [jax.ShapeDtypeStruct((nb, s, w), f32)])
    scratch = [pltpu.VMEM((nb, w, GLA_DK), f32), pltpu.VMEM((nb, tc + 8, w), f32), pltpu.VMEM((nb, 1, w), f32),
               pltpu.VMEM((nb, 1, 3 * w), f32), pltpu.VMEM((nb, 1, SM_W), f32), pltpu.VMEM((nb, w, w), f32),
               pltpu.VMEM((nb, tc + 8, 3 * w), f32), pltpu.VMEM((nb, SSD_DSTATE, w), f32)]
    outs = pl.pallas_call(
        functools.partial(_mix_body, names=tuple(names + out_names) + _MIX_SCRATCH, nb=nb, mix=mix),
        grid=(s // tc,),
        in_specs=in_specs,
        out_specs=[blk(w)] * len(out_names),
        out_shape=out_shape,
        scratch_shapes=scratch,
        compiler_params=pltpu.CompilerParams(
            dimension_semantics=("arbitrary",), vmem_limit_bytes=VMEM_LIMIT),
        name="mixers",
    )(*ins)
    return outs[:4], (v_first if mix else outs[4])


def _level_masks():
    n = 4 * RW_CHUNK
    t = np.arange(n)[:, None]
    u = np.arange(n)[None, :]
    out = []
    s = 1
    while s < RW_CHUNK:
        out.append(((t // (2 * s)) == (u // (2 * s))) & ((t % (2 * s)) >= s) & ((u % (2 * s)) < s))
        s *= 2
    return jnp.asarray(np.stack(out).astype(np.float32), dtype=bf16)


def _ones_blocks(seg):
    i = np.arange(W_GROUP)
    return jnp.asarray((i[:, None] // seg == i[None, :] // seg).astype(np.float32), dtype=bf16)


def _rows(a):
    return a[:, None, :]


def _pad_rows(m, row0):
    return jnp.pad(m, ((0, 0), (row0, SM_W - row0 - m.shape[1]), (0, 0)))


def _lane_rows(v, lane0):
    return jnp.pad(v, ((0, 0), (lane0, SM_W - lane0 - v.shape[1])))[:, None, :]


def _block_diag(w):
    nl, n, k, j = w.shape
    return jnp.einsum("lnkj,nm->lnkmj", w, jnp.eye(n, dtype=w.dtype)).reshape(nl, n * k, n * j)


def kernel(x, ffn1_norm, ffn1_w_gate, ffn1_w_up, ffn1_w_down, mix_norm, w_in, w_out, gla_alpha_up, gla_alpha_bias, gla_norm, lru_conv_w, lru_conv_b, lru_w_a, lru_b_a, lru_w_x, lru_b_x, lru_lambda, rw_mu, rw_w0, rw_w2, rw_a0, rw_a2, rw_g2, rw_v0, rw_v1, rw_v2, rw_k_k, rw_k_a, rw_r_k, rw_gn_w, rw_gn_b, ssd_conv_w, ssd_conv_b, ssd_dt_bias, ssd_a_log, ssd_d, ssd_norm, ffn2_norm, ffn2_w_gate, ffn2_w_up, ffn2_w_down, final_norm):
    nb, s, d = x.shape
    t = nb * s
    depth = w_in.shape[0]
    tm = min(FFN_TILE, t)

    wg1, wu1, wd1 = (a.astype(bf16) for a in (ffn1_w_gate, ffn1_w_up, ffn1_w_down))
    wg2, wu2, wd2 = (a.astype(bf16) for a in (ffn2_w_gate, ffn2_w_up, ffn2_w_down))
    w_out_b = w_out.astype(bf16)
    w_in_b = w_in.astype(bf16)
    w_in_p = jnp.concatenate([
        w_in_b[:, :, _GLA0:_GLA0 + 768], w_in_b[:, :, _LRU0:_LRU0 + 512], w_in_b[:, :, _RW0:_RW0 + 768],
        w_in_b[:, :, _SSD0:_SSD0 + 1024],
        w_in_b[:, :, _GLA0 + 768:_GLA0 + 784], w_in_b[:, :, _RW0 + 768:_RW0 + 832],
        w_in_b[:, :, _SSD0 + 1024:_SSD0 + 1028], jnp.zeros((depth, d, SM_W - 84), bf16)], axis=-1)
    rank_v = rw_v1.shape[-1]
    prm = dict(
        g_wup=_pad_rows(gla_alpha_up, SM_GLA), g_bias=_rows(gla_alpha_bias),
        g_nw=_rows(jnp.tile(gla_norm, (1, GLA_HEADS))),
        l_cw=lru_conv_w, l_cb=_rows(lru_conv_b),
        l_wax=jnp.concatenate([_block_diag(lru_w_a), _block_diag(lru_w_x)], axis=-1).astype(bf16),
        l_bax=_rows(jnp.concatenate([lru_b_a, lru_b_x], axis=-1)), l_lam=_rows(lru_lambda),
        r_mu=_rows(rw_mu[:, 0:768]), r_mus=_lane_rows(rw_mu[:, 768:832], SM_RW),
        r_w2=_pad_rows(rw_w2, SM_RW), r_a2=_pad_rows(rw_a2, SM_RW + 16), r_g2=_pad_rows(rw_g2, SM_RW + 32),
        r_vec=jnp.stack([rw_w0, rw_a0, rw_k_k, rw_k_a, rw_r_k.reshape(depth, -1), rw_gn_w, rw_gn_b,
                         jnp.pad(rw_v0, ((1, 0), (0, 0)))], axis=1),
        r_v1=jnp.pad(rw_v1, ((1, 0), (0, 0), (0, 128 - rank_v))),
        r_v2=jnp.pad(rw_v2, ((1, 0), (0, 128 - rank_v), (0, 0))),
        s_cw=ssd_conv_w, s_cb=_rows(ssd_conv_b), s_dtb=_lane_rows(ssd_dt_bias, SM_DT),
        s_alog=jnp.concatenate([_lane_rows(ssd_a_log, SM_DT), _lane_rows(jnp.ones_like(ssd_a_log), SM_DT)], axis=1),
        s_dsk=_rows(jnp.repeat(ssd_d, W_GROUP // SSD_HEADS, axis=-1)), s_nw=_rows(ssd_norm),
        ones=_ones_blocks(64), lvl=_level_masks())
    n1, n2, nm = _rows(ffn1_norm), _rows(ffn2_norm), _rows(mix_norm)

    x2 = x.reshape(t, d)
    v_first = None
    for l in range(depth):
        x2 = _ffn(x2, n1, wg1, wu1, wd1, l, tm=tm)
        ys, v_first = _mixers(x2.reshape(nb, s, d), nm, w_in_p, l, v_first, prm)
        ys = [a.reshape(t, W_GROUP) for a in ys]
        x2 = _ffn(x2, n2, wg2, wu2, wd2, l, tm=tm, ys=ys, w_out=w_out_b,
                  final_w=final_norm[None, :] if l == depth - 1 else None)
    return x2.reshape(nb, s, d)
```

```python
import functools
import math

import numpy as np
import jax
import jax.numpy as jnp
from jax import lax
from jax.experimental import pallas as pl
from jax.experimental.pallas import tpu as pltpu

f32 = jnp.float32
bf16 = jnp.bfloat16

D_MODEL = 1024
D_FF = 2816
W_GROUP = 256
NORM_EPS = 1e-6
GROUP_NORM_EPS = 1e-5
CONV_W = 4

GLA_HEADS = 4
GLA_DK = 128
GLA_HK = 32
GLA_HV = 64
GLA_GATE_NORM = 16.0
GLA_CHUNK = 64

LRU_C = 8.0

RW_HS = 64
RW_DECAY_SCALE = math.exp(-0.5)
RW_GN_EPS = 64e-5
RW_CHUNK = 64

SSD_HEADS = 4
SSD_DSTATE = 128
SSD_CHUNK = 128

_GLA0, _LRU0, _RW0, _SSD0 = 0, 784, 1296, 2128
_SEGS = dict(gla=(0, 768), lru=(768, 1280), rw=(1280, 2048), ssd=(2048, 3072), sm=(3072, 3200))
SM_GLA = 0
SM_RW = 16
SM_DT = 80
SM_W = 128

MIX_TILE = 256
FFN_TILE = 1024
MXU_TILE = 256
FFN_SPLITS = (MXU_TILE,) * (D_FF // MXU_TILE)
VMEM_LIMIT = 56 * 1024 * 1024


def _dot(a, b):
    return jnp.dot(a.astype(bf16), b.astype(bf16), preferred_element_type=f32)


def _dot_nt(a, b):
    return lax.dot_general(a.astype(bf16), b.astype(bf16), (((1,), (1,)), ((), ())),
                           preferred_element_type=f32)


def _dot_tn(a, b):
    return lax.dot_general(a.astype(bf16), b.astype(bf16), (((0,), (0,)), ((), ())),
                           preferred_element_type=f32)


def _seg_dot(x, ones_bf):
    return jnp.dot(x.astype(bf16), ones_bf, preferred_element_type=f32)


def _cumsum_rows(x, seg):
    row = lax.broadcasted_iota(jnp.int32, x.shape, 0) % seg
    s = 1
    while s < seg:
        x = x + jnp.where(row >= s, pltpu.roll(x, s, 0), 0.0)
        s *= 2
    return x


def _head_masks(rows, width):
    lane = lax.broadcasted_iota(jnp.int32, (rows, width), 1) // (width // 4)
    return [(lane == h).astype(f32).astype(bf16) for h in range(4)]


def _stack_heads(x, masks):
    xb = x.astype(bf16)
    return jnp.concatenate([xb * m for m in masks], axis=0)


def _expand_heads(x, lane0, width):
    n = x.shape[0]
    lane = lax.broadcasted_iota(jnp.int32, (n, width), 1) // (width // 4)
    out = jnp.zeros((n, width), f32)
    for h in range(4):
        out = jnp.where(lane == h, x[:, lane0 + h:lane0 + h + 1], out)
    return out


def _rms_norm(x, w):
    ms = jnp.mean(x * x, axis=-1, keepdims=True)
    return x * lax.rsqrt(ms + NORM_EPS) * w


def _merge(*stage_lists):
    keyed = []
    for k, stages in enumerate(stage_lists):
        total = float(sum(c for c, _ in stages))
        done = 0.0
        for i, (c, st) in enumerate(stages):
            keyed.append(((done + 0.5 * c) / total, k, i, st))
            done += c
    return [st for _, _, _, st in sorted(keyed, key=lambda t: t[:3])]


def _ffn_body(*refs, splits, proj, final):
    refs = list(refs)
    x_ref = refs.pop(0)
    if proj:
        y_refs = [refs.pop(0) for _ in range(4)]
        wo_ref = refs.pop(0)
    nw_ref, wg_ref, wu_ref, wd_ref = (refs.pop(0) for _ in range(4))
    if final:
        fw_ref = refs.pop(0)
    o_ref = refs.pop(0)

    x = x_ref[...]
    if proj:
        for i, y_ref in enumerate(y_refs):
            x = x + jnp.dot(y_ref[...], wo_ref[i * W_GROUP:(i + 1) * W_GROUP, :],
                            preferred_element_type=f32)
    xn = _rms_norm(x, nw_ref[...]).astype(bf16)
    acc = None
    lo = 0
    for width in splits:
        hi = lo + width
        g = jnp.dot(xn, wg_ref[:, lo:hi], preferred_element_type=f32)
        u = jnp.dot(xn, wu_ref[:, lo:hi], preferred_element_type=f32)
        h = (jax.nn.silu(g) * u).astype(bf16)
        part = jnp.dot(h, wd_ref[lo:hi, :], preferred_element_type=f32)
        acc = part if acc is None else acc + part
        lo = hi
    out = x + 0.5 * acc
    if final:
        out = _rms_norm(out, fw_ref[...])
    o_ref[...] = out


def _ffn(x2, nw, wg, wu, wd, layer, *, tm, ys=None, w_out=None, final_w=None):
    t, d = x2.shape
    f = wg.shape[-1]
    proj = ys is not None
    final = final_w is not None
    tile = lambda width: pl.BlockSpec((tm, width), lambda i: (i, 0))
    row = pl.BlockSpec((1, d), lambda i: (0, 0))
    row_of_layer = pl.BlockSpec((None, 1, d), lambda i: (layer, 0, 0))
    resident = lambda r, c: pl.BlockSpec((None, r, c), lambda i: (layer, 0, 0), pipeline_mode=pl.Buffered(1))
    ins = [x2] + (list(ys) + [w_out] if proj else []) + [nw, wg, wu, wd] + ([final_w] if final else [])
    in_specs = [tile(d)] + ([tile(W_GROUP)] * 4 + [resident(d, d)] if proj else []) \
        + [row_of_layer, resident(d, f), resident(d, f), resident(f, d)] + ([row] if final else [])
    return pl.pallas_call(
        functools.partial(_ffn_body, splits=FFN_SPLITS, proj=proj, final=final),
        grid=(t // tm,),
        in_specs=in_specs,
        out_specs=tile(d),
        out_shape=jax.ShapeDtypeStruct((t, d), f32),
        compiler_params=pltpu.CompilerParams(
            dimension_semantics=("parallel",), vmem_limit_bytes=VMEM_LIMIT),
        name="ffn",
    )(*ins)


def _gla_stages(R, P, nb):
    L = GLA_CHUNK
    tc = MIX_TILE
    nch = tc // L
    row = lax.broadcasted_iota(jnp.int32, (L, 4 * L), 0)
    col = lax.broadcasted_iota(jnp.int32, (L, 4 * L), 1)
    causal = (col % L) <= row
    srow = lax.broadcasted_iota(jnp.int32, (W_GROUP, GLA_DK), 0) // GLA_HV
    scol = lax.broadcasted_iota(jnp.int32, (W_GROUP, GLA_DK), 1) // GLA_HK
    blockdiag = srow == scol
    k_masks = _head_masks(L, GLA_DK)
    v_masks = _head_masks(L, W_GROUP)
    pre, intra, kv, dec = {}, {}, {}, {}
    st = {}
    outs = [[None] * nch for _ in range(nb)]

    def prep(b):
        x = P[b]["gla"]
        z = _dot(P[b]["sm"], R["g_wup"][...]) + R["g_bias"][...]
        bc = _cumsum_rows(jax.nn.log_sigmoid(z) / GLA_GATE_NORM, L)
        q_dec = (x[:, 0:128] * (GLA_HK ** -0.5) * jnp.exp(bc)).astype(bf16)
        pre[b] = (x, bc, q_dec, x[:, 128:256] * jnp.exp(-bc))
        st[b] = R["g_st"][b]

    def chain(j, b):
        x, bc, q_dec, k_dec = pre[b]
        rows = slice(j * L, (j + 1) * L)
        s_cat = _dot_nt(q_dec[rows], _stack_heads(k_dec[rows], k_masks))
        s_cat = jnp.where(causal, s_cat, 0.0)
        v = x[rows, 256:512]
        intra[j, b] = _dot(s_cat, _stack_heads(v, v_masks))
        b_last = bc[(j + 1) * L - 1:(j + 1) * L, :]
        k_end = x[rows, 128:256] * jnp.exp(b_last - bc[rows])
        kv[j, b] = jnp.where(blockdiag, _dot_tn(v, k_end), 0.0)
        dec[j, b] = jnp.exp(b_last)

    def link(j, b):
        q_dec = pre[b][2]
        outs[b][j] = intra[j, b] + _dot_nt(q_dec[j * L:(j + 1) * L], st[b])
        st[b] = st[b] * dec[j, b] + kv[j, b]

    def finish(b):
        R["g_st"][b] = st[b]
        o = jnp.concatenate(outs[b], axis=0)
        ms = _seg_dot(o * o, R["ones"][...]) * (1.0 / GLA_HV)
        o = o * lax.rsqrt(ms + GROUP_NORM_EPS) * R["g_nw"][...]
        R["o_gla"][b] = (o * jax.nn.silu(pre[b][0][:, 512:768])).astype(bf16)

    P_ = functools.partial
    chains = [(j, b) for j in range(nch) for b in range(nb)]
    return ([(350, P_(prep, b)) for b in range(nb)] + [(150, P_(chain, j, b)) for j, b in chains]
            + [(80, P_(link, j, b)) for j, b in chains] + [(300, P_(finish, b)) for b in range(nb)])


def _lru_stages(R, P, nb):
    tc = MIX_TILE
    w = W_GROUP
    row = lax.broadcasted_iota(jnp.int32, (tc, w), 0)
    au = {}

    def gates(b):
        x = P[b]["lru"]
        xb = x[:, 0:w]
        ext = R["l_ext"]
        ext[b, 8:, :] = xb
        xc = R["l_cb"][...] + R["l_cw"][CONV_W - 1:CONV_W, :] * xb
        for j in range(1, CONV_W):
            xc = xc + R["l_cw"][CONV_W - 1 - j:CONV_W - j, :] * ext[b, pl.ds(8 - j, tc), :]
        ext[b, 0:8, :] = xb[tc - 8:tc, :]
        ri = jax.nn.sigmoid(_dot(xc, R["l_wax"][...]) + R["l_bax"][...])
        log_a = -LRU_C * ri[:, 0:w] * jax.nn.softplus(-R["l_lam"][...])
        a = jnp.exp(log_a)
        au[b] = (a, jnp.sqrt(1.0 - a * a) * (ri[:, w:2 * w] * xc))

    def scan(b, shifts):
        a, u = au[b]
        for s in shifts:
            keep = row >= s
            u = u + a * jnp.where(keep, pltpu.roll(u, s, 0), 0.0)
            a = a * jnp.where(keep, pltpu.roll(a, s, 0), 1.0)
        au[b] = (a, u)

    def finish(b):
        a, u = au[b]
        h = u + a * R["l_h"][b]
        R["l_h"][b] = h[tc - 1:tc, :]
        R["o_lru"][b] = (h * jax.nn.gelu(P[b]["lru"][:, w:2 * w])).astype(bf16)

    shifts = [1 << i for i in range(tc.bit_length() - 1)]
    P_ = functools.partial
    out = []
    for b in range(nb):
        out += [(700, P_(gates, b))] + [(260, P_(scan, b, shifts[i:i + 2])) for i in range(0, len(shifts), 2)] \
            + [(250, P_(finish, b))]
    return out


def _rw_stages(R, P, nb, mix):
    L = RW_CHUNK
    tc = MIX_TILE
    nch = tc // L
    w = W_GROUP
    n4 = 4 * L
    row3 = lax.broadcasted_iota(jnp.int32, (tc, 3 * w), 0)
    rows_sm = lax.broadcasted_iota(jnp.int32, (tc, SM_W), 0)
    rr = lax.broadcasted_iota(jnp.int32, (n4, n4), 0)
    cc = lax.broadcasted_iota(jnp.int32, (n4, n4), 1)
    strict = (rr % L) > (cc % L)
    incl = (rr % L) >= (cc % L)
    eye = (rr == cc).astype(f32).astype(bf16)
    masks = _head_masks(L, w)
    w0, a0, k_k, k_a, r_k, gn_w, gn_b, v0 = (R["r_vec"][i:i + 1, :] for i in range(8))
    lvl = R["lvl"]
    pre, hs = {}, {}
    ch = {}
    ys = [[None] * nch for _ in range(nb)]
    keys = [(j, b) for j in range(nch) for b in range(nb)]

    def prep_a(b):
        x = P[b]["rw"]
        sm = P[b]["sm"]
        x_prev = jnp.where(row3 == 0, R["r_cx"][b], pltpu.roll(x, 1, 0))
        sm_prev = jnp.where(rows_sm == 0, R["r_cs"][b], pltpu.roll(sm, 1, 0))
        R["r_cx"][b] = x[tc - 1:tc, :]
        R["r_cs"][b] = sm[tc - 1:tc, :]
        x = x + (x_prev - x) * R["r_mu"][...]
        sm = sm + (sm_prev - sm) * R["r_mus"][...]
        r = x[:, 0:w]
        k = x[:, w:2 * w]
        v = x[:, 2 * w:3 * w]
        log_w = -RW_DECAY_SCALE * jax.nn.sigmoid(w0 + _dot(jnp.tanh(sm), R["r_w2"][...]))
        a = jax.nn.sigmoid(a0 + _dot(sm, R["r_a2"][...]))
        g = _dot(jax.nn.sigmoid(sm), R["r_g2"][...])
        if mix:
            lam = jax.nn.sigmoid(v0 + _dot(_dot(v, R["r_v1"][...]), R["r_v2"][...]))
            v = v + (R["vf"][b] - v) * lam
        else:
            R["o_v"][b] = v
        pre[b] = dict(r=r, k=k, v=v, g=g, a=a, log_w=log_w)

    def prep_b(b):
        p = pre[b]
        r, k, v, g, a, log_w = (p[n] for n in ("r", "k", "v", "g", "a", "log_w"))
        kk = k * k_k
        n2 = _seg_dot(kk * kk, R["ones"][...])
        kk = kk / jnp.maximum(jnp.sqrt(n2), 1e-12)
        k = k * (1.0 + (a - 1.0) * k_a)
        cum = _cumsum_rows(log_w, L)
        g_in = jnp.exp(cum)
        g_inv = jnp.exp(-cum)
        pre[b] = dict(r=r, k=k, v=v, g=g, g_in=g_in,
                      al=-kk * jnp.exp(cum - log_w),
                      be=kk * a * g_inv, kt=k * g_inv, rt=r * g_in)
        hs[b] = R["r_h"][b]

    def scores(j, b):
        p = pre[b]
        rows = slice(j * L, (j + 1) * L)
        g_last = p["g_in"][(j + 1) * L - 1:(j + 1) * L, :]
        d = dict()
        d["g_col"] = jnp.broadcast_to(jnp.broadcast_to(g_last, (8, w)).T[:, 0:1], (w, w))
        d["al"] = _stack_heads(p["al"][rows], masks)
        d["rt"] = _stack_heads(p["rt"][rows], masks)
        d["vst"] = _stack_heads(p["v"][rows], masks)
        be = _stack_heads(p["be"][rows], masks)
        kt = _stack_heads(p["kt"][rows], masks)
        d["be_end"] = _stack_heads(p["be"][rows] * g_last, masks)
        kt_end = _stack_heads(p["kt"][rows] * g_last, masks)
        aa = _dot_nt(jnp.concatenate([d["al"], d["rt"]], axis=0), jnp.concatenate([be, kt], axis=0))
        d["a_ab"] = jnp.where(strict, aa[0:n4, 0:n4], 0.0).astype(bf16)
        d["a_ak"] = jnp.where(strict, aa[0:n4, n4:2 * n4], 0.0).astype(bf16)
        d["a_rb"] = jnp.where(incl, aa[n4:2 * n4, 0:n4], 0.0).astype(bf16)
        d["a_rk"] = jnp.where(incl, aa[n4:2 * n4, n4:2 * n4], 0.0).astype(bf16)
        d["kv"] = _dot_tn(kt_end, d["vst"])
        ch[j, b] = d

    def level0():
        for key in keys:
            ch[key]["t"] = eye + ch[key]["a_ab"] * lvl[0]

    def level_a(lv, part):
        s = 1 << lv
        for key in part:
            t = ch[key]["t"]
            if s >= 16:
                t = jnp.concatenate([t[r:r + s] for r in range(s, n4, 2 * s)], axis=0)
            ch[key]["x1"] = _dot(t, ch[key]["a_ab"] * lvl[lv])

    def level_b(lv, part):
        s = 1 << lv
        for key in part:
            t = ch[key]["t"]
            x = _dot(ch[key]["x1"], t).astype(bf16)
            if s >= 16:
                pieces = []
                for i, r in enumerate(range(0, n4, 2 * s)):
                    pieces += [t[r:r + s], t[r + s:r + 2 * s] + x[i * s:(i + 1) * s]]
                ch[key]["t"] = jnp.concatenate(pieces, axis=0)
            else:
                ch[key]["t"] = t + x

    def solve(j, b):
        d = ch[j, b]
        ak = _dot(jnp.concatenate([d["a_ak"], d["a_rk"]], axis=0), d["vst"])
        pq = _dot(d["t"], jnp.concatenate([d["al"], ak[0:n4].astype(bf16)], axis=1))
        d["p_st"] = pq[:, 0:w].astype(bf16)
        d["q_st"] = pq[:, w:2 * w]
        d["y_k"] = ak[n4:2 * n4]

    def link(j, b):
        d = ch[j, b]
        h_bf = hs[b].astype(bf16)
        u = (_dot(d["p_st"], h_bf) + d["q_st"]).astype(bf16)
        hs[b] = hs[b] * d["g_col"] + _dot_tn(d["be_end"], u) + d["kv"]
        y_st = _dot(d["rt"], h_bf) + _dot(d["a_rb"], u) + d["y_k"]
        ys[b][j] = y_st[0:L] + y_st[L:2 * L] + y_st[2 * L:3 * L] + y_st[3 * L:4 * L]

    def finish(b):
        R["r_h"][b] = hs[b]
        p = pre[b]
        y = jnp.concatenate(ys[b], axis=0)
        mean = _seg_dot(y, R["ones"][...]) * (1.0 / RW_HS)
        dlt = y - mean
        var = _seg_dot(dlt * dlt, R["ones"][...]) * (1.0 / RW_HS)
        yn = dlt * lax.rsqrt(var + RW_GN_EPS) * gn_w + gn_b
        bonus = _seg_dot(p["r"] * p["k"] * r_k, R["ones"][...]) * p["v"]
        R["o_rw"][b] = ((yn + bonus) * p["g"]).astype(bf16)

    P_ = functools.partial
    n_lvl = RW_CHUNK.bit_length() - 1
    halves = (keys[:len(keys) // 2], keys[len(keys) // 2:])
    return ([st for b in range(nb) for st in ((900, P_(prep_a, b)), (900, P_(prep_b, b)))],
            [(450, P_(scores, j, b)) for j, b in keys] + [(150, level0)]
            + [(300, st) for lv in range(1, n_lvl) for st in
               (P_(level_a, lv, halves[0]), P_(level_a, lv, halves[1]), P_(level_b, lv, halves[0]), P_(level_b, lv, halves[1]))]
            + [(350, P_(solve, j, b)) for j, b in keys],
            [(300, P_(link, j, b)) for j, b in keys] + [(500, P_(finish, b)) for b in range(nb)])


def _ssd_stages(R, P, nb):
    L = SSD_CHUNK
    tc = MIX_TILE
    nch = tc // L
    w = W_GROUP
    cdim = 3 * w
    ll = lax.broadcasted_iota(jnp.int32, (L, L), 0)
    ss = lax.broadcasted_iota(jnp.int32, (L, L), 1)
    causal = ll >= ss
    lane_head = lax.broadcasted_iota(jnp.int32, (L, w), 1) // (w // SSD_HEADS)
    pre, hst = {}, {}
    ys = [[None] * nch for _ in range(nb)]

    def prep(b):
        x = P[b]["ssd"]
        xbc = x[:, w:4 * w]
        ext = R["s_ext"]
        ext[b, 8:, :] = xbc
        conv = R["s_cb"][...] + R["s_cw"][CONV_W - 1:CONV_W, :] * xbc
        for j in range(1, CONV_W):
            conv = conv + R["s_cw"][CONV_W - 1 - j:CONV_W - j, :] * ext[b, pl.ds(8 - j, tc), :]
        ext[b, 0:8, :] = xbc[tc - 8:tc, :]
        xbc = jax.nn.silu(conv)
        dt = jax.nn.softplus(P[b]["sm"] + R["s_dtb"][...])
        d_a = dt * (-jnp.exp(R["s_alog"][0:1, :]) * R["s_alog"][1:2, :])
        cs = _cumsum_rows(d_a, L)
        pre[b] = dict(xbc=xbc, cs=cs, dt_full=_expand_heads(dt, SM_DT, w), cs_full=_expand_heads(cs, SM_DT, w))
        hst[b] = R["s_h"][b]

    cur = {}

    def chunk_pre(j, b):
        p = pre[b]
        rows = slice(j * L, (j + 1) * L)
        xs = p["xbc"][rows, 0:w]
        cs = p["cs"][rows]
        cs_full = p["cs_full"][rows]
        cs_last = cs_full[L - 1:L, :]
        xdt = xs * p["dt_full"][rows]
        cur[b] = dict(xs=xs, bm=p["xbc"][rows, w:2 * w], cm=p["xbc"][rows, 2 * w:3 * w], cs=cs, cs_t=cs.T,
                      cs_full=cs_full, cs_last=cs_last, xdt=xdt, xw=xdt * jnp.exp(cs_last - cs_full),
                      y=jnp.zeros((L, w), f32), y_off=[], st=[])
        cur[b]["bm_t"] = cur[b]["bm"].T

    def group(b, grp):
        q = cur[b]
        lo, hi = grp * SSD_DSTATE, (grp + 1) * SSD_DSTATE
        cb_g = _dot_nt(q["cm"][:, lo:hi], q["bm"][:, lo:hi])
        for r in range(2):
            hd = grp * 2 + r
            seg = q["cs"][:, SM_DT + hd:SM_DT + hd + 1] - q["cs_t"][SM_DT + hd:SM_DT + hd + 1, :]
            m = cb_g * jnp.where(causal, jnp.exp(seg), 0.0)
            q["y"] = q["y"] + _dot(m, jnp.where(lane_head == hd, q["xdt"], 0.0))
        q["y_off"].append(_dot(q["cm"][:, lo:hi], hst[b][:, lo:hi]))
        q["st"].append(_dot(q["bm_t"][lo:hi, :], q["xw"][:, lo:hi]))

    def chunk_fin(j, b):
        q = cur[b]
        y = q["y"] + jnp.concatenate(q["y_off"], axis=1) * jnp.exp(q["cs_full"])
        hst[b] = hst[b] * jnp.exp(q["cs_last"]) + jnp.concatenate(q["st"], axis=1)
        ys[b][j] = y + R["s_dsk"][...] * q["xs"]

    def finish(b):
        R["s_h"][b] = hst[b]
        y = jnp.concatenate(ys[b], axis=0) * jax.nn.silu(P[b]["ssd"][:, 0:w])
        outs = []
        for grp in range(2):
            yg = y[:, grp * 128:(grp + 1) * 128]
            ms = jnp.mean(yg * yg, axis=-1, keepdims=True)
            outs.append(yg * lax.rsqrt(ms + GROUP_NORM_EPS))
        R["o_ssd"][b] = (jnp.concatenate(outs, axis=1) * R["s_nw"][...]).astype(bf16)

    P_ = functools.partial
    out = []
    for b in range(nb):
        out += [(1000, P_(prep, b))]
        for j in range(nch):
            out += [(200, P_(chunk_pre, j, b)), (350, P_(group, b, 0)), (350, P_(group, b, 1)), (150, P_(chunk_fin, j, b))]
        out += [(350, P_(finish, b))]
    return out


_MIX_IN = ("x", "nw", "w_in", "g_wup", "g_bias", "g_nw", "l_cw", "l_cb", "l_wax", "l_bax", "l_lam",
           "r_mu", "r_mus", "r_w2", "r_a2", "r_g2", "r_vec", "s_cw", "s_cb", "s_dtb", "s_alog", "s_dsk", "s_nw",
           "ones", "lvl")
_MIX_SCRATCH = ("g_st", "l_ext", "l_h", "r_cx", "r_cs", "r_h", "s_ext", "s_h")


def _mix_body(*refs, names, nb, mix):
    R = dict(zip(names, refs))
    c = pl.program_id(0)

    @pl.when(c == 0)
    def _():
        for name in ("g_st", "l_h", "r_cx", "r_cs", "r_h", "s_h"):
            R[name][...] = jnp.zeros_like(R[name])
        for b in range(nb):
            R["l_ext"][b, 0:8, :] = jnp.zeros((8, W_GROUP), f32)
            R["s_ext"][b, 0:8, :] = jnp.zeros((8, 3 * W_GROUP), f32)

    P = [dict() for _ in range(nb)]
    xn = {}

    def norm(b):
        xn[b] = _rms_norm(R["x"][b], R["nw"][...]).astype(bf16)

    def project(b, seg):
        lo, hi = _SEGS[seg]
        P[b][seg] = jnp.dot(xn[b], R["w_in"][:, lo:hi], preferred_element_type=f32)

    P_ = functools.partial
    batches = range(nb)
    rw_prep, rw_chain, rw_tail = _rw_stages(R, P, nb, mix)
    head = [P_(norm, b) for b in batches] + [P_(project, b, seg) for b in batches for seg in ("rw", "sm")]
    rest_proj = [((_SEGS[seg][1] - _SEGS[seg][0]), P_(project, b, seg)) for seg in ("ssd", "lru", "gla") for b in batches]
    others = _ssd_stages(R, P, nb) + _lru_stages(R, P, nb) + _gla_stages(R, P, nb)
    for stage in head + _merge(rw_prep, rest_proj) + _merge(rw_chain + rw_tail, others):
        stage()


def _mixers(x3, nw, w_in_p, layer, v_first, prm):
    nb, s, d = x3.shape
    tc = MIX_TILE
    w = W_GROUP
    mix = v_first is not None
    blk = lambda width: pl.BlockSpec((nb, tc, width), lambda c: (0, c, 0))
    full = lambda a: pl.BlockSpec(a.shape, lambda c: (0,) * a.ndim)
    of_layer = lambda a: pl.BlockSpec((None,) + a.shape[1:], lambda c: (layer,) + (0,) * (a.ndim - 1))
    names = list(_MIX_IN) + (["r_v1", "r_v2", "vf"] if mix else [])
    arrays = dict(prm, x=x3, nw=nw, w_in=w_in_p, vf=v_first)
    specs = dict(x=blk(d), vf=blk(w), ones=full(prm["ones"]), lvl=full(prm["lvl"]),
                 w_in=pl.BlockSpec((None, d, w_in_p.shape[-1]), lambda c: (layer, 0, 0),
                                   pipeline_mode=pl.Buffered(1)))
    ins = [arrays[n] for n in names]
    in_specs = [specs[n] if n in specs else of_layer(arrays[n]) for n in names]
    out_names = ["o_gla", "o_lru", "o_rw", "o_ssd"] + ([] if mix else ["o_v"])
    out_shape = [jax.ShapeDtypeStruct((nb, s, w), bf16)] * 4 + ([] if mix else [jax.ShapeDtypeStruct((nb, s, w), f32)])
    scratch = [pltpu.VMEM((nb, w, GLA_DK), f32), pltpu.VMEM((nb, tc + 8, w), f32), pltpu.VMEM((nb, 1, w), f32),
               pltpu.VMEM((nb, 1, 3 * w), f32), pltpu.VMEM((nb, 1, SM_W), f32), pltpu.VMEM((nb, w, w), f32),
               pltpu.VMEM((nb, tc + 8, 3 * w), f32), pltpu.VMEM((nb, SSD_DSTATE, w), f32)]
    outs = pl.pallas_call(
        functools.partial(_mix_body, names=tuple(names + out_names) + _MIX_SCRATCH, nb=nb, mix=mix),
        grid=(s // tc,),
        in_specs=in_specs,
        out_specs=[blk(w)] * len(out_names),
        out_shape=out_shape,
        scratch_shapes=scratch,
        compiler_params=pltpu.CompilerParams(
            dimension_semantics=("arbitrary",), vmem_limit_bytes=VMEM_LIMIT),
        name="mixers",
    )(*ins)
    return outs[:4], (v_first if mix else outs[4])


def _level_masks():
    n = 4 * RW_CHUNK
    t = np.arange(n)[:, None]
    u = np.arange(n)[None, :]
    out = []
    s = 1
    while s < RW_CHUNK:
        out.append(((t // (2 * s)) == (u // (2 * s))) & ((t % (2 * s)) >= s) & ((u % (2 * s)) < s))
        s *= 2
    return jnp.asarray(np.stack(out).astype(np.float32), dtype=bf16)


def _ones_blocks(seg):
    i = np.arange(W_GROUP)
    return jnp.asarray((i[:, None] // seg == i[None, :] // seg).astype(np.float32), dtype=bf16)


def _rows(a):
    return a[:, None, :]


def _pad_rows(m, row0):
    return jnp.pad(m, ((0, 0), (row0, SM_W - row0 - m.shape[1]), (0, 0)))


def _lane_rows(v, lane0):
    return jnp.pad(v, ((0, 0), (lane0, SM_W - lane0 - v.shape[1])))[:, None, :]


def _block_diag(w):
    nl, n, k, j = w.shape
    return jnp.einsum("lnkj,nm->lnkmj", w, jnp.eye(n, dtype=w.dtype)).reshape(nl, n * k, n * j)


def kernel(x, ffn1_norm, ffn1_w_gate, ffn1_w_up, ffn1_w_down, mix_norm, w_in, w_out, gla_alpha_up, gla_alpha_bias, gla_norm, lru_conv_w, lru_conv_b, lru_w_a, lru_b_a, lru_w_x, lru_b_x, lru_lambda, rw_mu, rw_w0, rw_w2, rw_a0, rw_a2, rw_g2, rw_v0, rw_v1, rw_v2, rw_k_k, rw_k_a, rw_r_k, rw_gn_w, rw_gn_b, ssd_conv_w, ssd_conv_b, ssd_dt_bias, ssd_a_log, ssd_d, ssd_norm, ffn2_norm, ffn2_w_gate, ffn2_w_up, ffn2_w_down, final_norm):
    nb, s, d = x.shape
    t = nb * s
    depth = w_in.shape[0]
    tm = min(FFN_TILE, t)

    wg1, wu1, wd1 = (a.astype(bf16) for a in (ffn1_w_gate, ffn1_w_up, ffn1_w_down))
    wg2, wu2, wd2 = (a.astype(bf16) for a in (ffn2_w_gate, ffn2_w_up, ffn2_w_down))
    w_out_b = w_out.astype(bf16)
    w_in_b = w_in.astype(bf16)
    w_in_p = jnp.concatenate([
        w_in_b[:, :, _GLA0:_GLA0 + 768], w_in_b[:, :, _LRU0:_LRU0 + 512], w_in_b[:, :, _RW0:_RW0 + 768],
        w_in_b[:, :, _SSD0:_SSD0 + 1024],
        w_in_b[:, :, _GLA0 + 768:_GLA0 + 784], w_in_b[:, :, _RW0 + 768:_RW0 + 832],
        w_in_b[:, :, _SSD0 + 1024:_SSD0 + 1028], jnp.zeros((depth, d, SM_W - 84), bf16)], axis=-1)
    rank_v = rw_v1.shape[-1]
    prm = dict(
        g_wup=_pad_rows(gla_alpha_up, SM_GLA), g_bias=_rows(gla_alpha_bias),
        g_nw=_rows(jnp.tile(gla_norm, (1, GLA_HEADS))),
        l_cw=lru_conv_w, l_cb=_rows(lru_conv_b),
        l_wax=jnp.concatenate([_block_diag(lru_w_a), _block_diag(lru_w_x)], axis=-1).astype(bf16),
        l_bax=_rows(jnp.concatenate([lru_b_a, lru_b_x], axis=-1)), l_lam=_rows(lru_lambda),
        r_mu=_rows(rw_mu[:, 0:768]), r_mus=_lane_rows(rw_mu[:, 768:832], SM_RW),
        r_w2=_pad_rows(rw_w2, SM_RW), r_a2=_pad_rows(rw_a2, SM_RW + 16), r_g2=_pad_rows(rw_g2, SM_RW + 32),
        r_vec=jnp.stack([rw_w0, rw_a0, rw_k_k, rw_k_a, rw_r_k.reshape(depth, -1), rw_gn_w, rw_gn_b,
                         jnp.pad(rw_v0, ((1, 0), (0, 0)))], axis=1),
        r_v1=jnp.pad(rw_v1, ((1, 0), (0, 0), (0, 128 - rank_v))),
        r_v2=jnp.pad(rw_v2, ((1, 0), (0, 128 - rank_v), (0, 0))),
        s_cw=ssd_conv_w, s_cb=_rows(ssd_conv_b), s_dtb=_lane_rows(ssd_dt_bias, SM_DT),
        s_alog=jnp.concatenate([_lane_rows(ssd_a_log, SM_DT), _lane_rows(jnp.ones_like(ssd_a_log), SM_DT)], axis=1),
        s_dsk=_rows(jnp.repeat(ssd_d, W_GROUP // SSD_HEADS, axis=-1)), s_nw=_rows(ssd_norm),
        ones=_ones_blocks(64), lvl=_level_masks())
    n1, n2, nm = _rows(ffn1_norm), _rows(ffn2_norm), _rows(mix_norm)

    x2 = x.reshape(t, d)
    v_first = None
    for l in range(depth):
        x2 = _ffn(x2, n1, wg1, wu1, wd1, l, tm=tm)
        ys, v_first = _mixers(x2.reshape(nb, s, d), nm, w_in_p, l, v_first, prm)
        ys = [a.reshape(t, W_GROUP) for a in ys]
        x2 = _ffn(x2, n2, wg2, wu2, wd2, l, tm=tm, ys=ys, w_out=w_out_b,
                  final_w=final_norm[None, :] if l == depth - 1 else None)
    return x2.reshape(nb, s, d)
```

```python
import functools
import math

import numpy as np
import jax
import jax.numpy as jnp
from jax import lax
from jax.experimental import pallas as pl
from jax.experimental.pallas import tpu as pltpu

f32 = jnp.float32
bf16 = jnp.bfloat16

D_MODEL = 1024
D_FF = 2816
W_GROUP = 256
NORM_EPS = 1e-6
GROUP_NORM_EPS = 1e-5
CONV_W = 4

GLA_HEADS = 4
GLA_DK = 128
GLA_HK = 32
GLA_HV = 64
GLA_GATE_NORM = 16.0
GLA_CHUNK = 64

LRU_C = 8.0

RW_HS = 64
RW_DECAY_SCALE = math.exp(-0.5)
RW_GN_EPS = 64e-5
RW_CHUNK = 64

SSD_HEADS = 4
SSD_DSTATE = 128
SSD_CHUNK = 128

_GLA0, _LRU0, _RW0, _SSD0 = 0, 784, 1296, 2128
_SEGS = dict(gla=(0, 768), lru=(768, 1280), rw=(1280, 2048), ssd=(2048, 3072), sm=(3072, 3200))
SM_GLA = 0
SM_RW = 16
SM_DT = 80
SM_W = 128

MIX_TILE = 128
FFN_TILE = 1024
MXU_TILE = 256
FFN_SPLITS = (MXU_TILE,) * (D_FF // MXU_TILE)
VMEM_LIMIT = 56 * 1024 * 1024


def _dot(a, b):
    return jnp.dot(a.astype(bf16), b.astype(bf16), preferred_element_type=f32)


def _dot_nt(a, b):
    return lax.dot_general(a.astype(bf16), b.astype(bf16), (((1,), (1,)), ((), ())),
                           preferred_element_type=f32)


def _dot_tn(a, b):
    return lax.dot_general(a.astype(bf16), b.astype(bf16), (((0,), (0,)), ((), ())),
                           preferred_element_type=f32)


def _seg_dot(x, ones_bf):
    return jnp.dot(x.astype(bf16), ones_bf, preferred_element_type=f32)


def _cumsum_rows(x, seg):
    row = lax.broadcasted_iota(jnp.int32, x.shape, 0) % seg
    s = 1
    while s < seg:
        x = x + jnp.where(row >= s, pltpu.roll(x, s, 0), 0.0)
        s *= 2
    return x


def _head_masks(rows, width):
    lane = lax.broadcasted_iota(jnp.int32, (rows, width), 1) // (width // 4)
    return [(lane == h).astype(f32).astype(bf16) for h in range(4)]


def _stack_heads(x, masks):
    xb = x.astype(bf16)
    return jnp.concatenate([xb * m for m in masks], axis=0)


def _expand_heads(x, lane0, width):
    n = x.shape[0]
    lane = lax.broadcasted_iota(jnp.int32, (n, width), 1) // (width // 4)
    out = jnp.zeros((n, width), f32)
    for h in range(4):
        out = jnp.where(lane == h, x[:, lane0 + h:lane0 + h + 1], out)
    return out


def _rms_norm(x, w):
    ms = jnp.mean(x * x, axis=-1, keepdims=True)
    return x * lax.rsqrt(ms + NORM_EPS) * w


def _merge(*stage_lists):
    keyed = []
    for k, stages in enumerate(stage_lists):
        total = float(sum(c for c, _ in stages))
        done = 0.0
        for i, (c, st) in enumerate(stages):
            keyed.append(((done + 0.5 * c) / total, k, i, st))
            done += c
    return [st for _, _, _, st in sorted(keyed, key=lambda t: t[:3])]


def _ffn_body(*refs, splits, proj, final):
    refs = list(refs)
    x_ref = refs.pop(0)
    if proj:
        y_refs = [refs.pop(0) for _ in range(4)]
        wo_ref = refs.pop(0)
    nw_ref, wg_ref, wu_ref, wd_ref = (refs.pop(0) for _ in range(4))
    if final:
        fw_ref = refs.pop(0)
    o_ref = refs.pop(0)

    x = x_ref[...]
    if proj:
        for i, y_ref in enumerate(y_refs):
            x = x + jnp.dot(y_ref[...], wo_ref[i * W_GROUP:(i + 1) * W_GROUP, :],
                            preferred_element_type=f32)
    xn = _rms_norm(x, nw_ref[...]).astype(bf16)
    acc = None
    lo = 0
    for width in splits:
        hi = lo + width
        g = jnp.dot(xn, wg_ref[:, lo:hi], preferred_element_type=f32)
        u = jnp.dot(xn, wu_ref[:, lo:hi], preferred_element_type=f32)
        h = (jax.nn.silu(g) * u).astype(bf16)
        part = jnp.dot(h, wd_ref[lo:hi, :], preferred_element_type=f32)
        acc = part if acc is None else acc + part
        lo = hi
    out = x + 0.5 * acc
    if final:
        out = _rms_norm(out, fw_ref[...])
    o_ref[...] = out


def _ffn(x2, nw, wg, wu, wd, layer, *, tm, ys=None, w_out=None, final_w=None):
    t, d = x2.shape
    f = wg.shape[-1]
    proj = ys is not None
    final = final_w is not None
    tile = lambda width: pl.BlockSpec((tm, width), lambda i: (i, 0))
    row = pl.BlockSpec((1, d), lambda i: (0, 0))
    row_of_layer = pl.BlockSpec((None, 1, d), lambda i: (layer, 0, 0))
    resident = lambda r, c: pl.BlockSpec((None, r, c), lambda i: (layer, 0, 0), pipeline_mode=pl.Buffered(1))
    ins = [x2] + (list(ys) + [w_out] if proj else []) + [nw, wg, wu, wd] + ([final_w] if final else [])
    in_specs = [tile(d)] + ([tile(W_GROUP)] * 4 + [resident(d, d)] if proj else []) \
        + [row_of_layer, resident(d, f), resident(d, f), resident(f, d)] + ([row] if final else [])
    return pl.pallas_call(
        functools.partial(_ffn_body, splits=FFN_SPLITS, proj=proj, final=final),
        grid=(t // tm,),
        in_specs=in_specs,
        out_specs=tile(d),
        out_shape=jax.ShapeDtypeStruct((t, d), f32),
        compiler_params=pltpu.CompilerParams(
            dimension_semantics=("parallel",), vmem_limit_bytes=VMEM_LIMIT),
        name="ffn",
    )(*ins)


def _gla_stages(R, P, nb):
    L = GLA_CHUNK
    tc = MIX_TILE
    nch = tc // L
    row = lax.broadcasted_iota(jnp.int32, (L, 4 * L), 0)
    col = lax.broadcasted_iota(jnp.int32, (L, 4 * L), 1)
    causal = (col % L) <= row
    srow = lax.broadcasted_iota(jnp.int32, (W_GROUP, GLA_DK), 0) // GLA_HV
    scol = lax.broadcasted_iota(jnp.int32, (W_GROUP, GLA_DK), 1) // GLA_HK
    blockdiag = srow == scol
    k_masks = _head_masks(L, GLA_DK)
    v_masks = _head_masks(L, W_GROUP)
    pre, intra, kv, dec = {}, {}, {}, {}
    st = {}
    outs = [[None] * nch for _ in range(nb)]

    def prep(b):
        x = P[b]["gla"]
        z = _dot(P[b]["sm"], R["g_wup"][...]) + R["g_bias"][...]
        bc = _cumsum_rows(jax.nn.log_sigmoid(z) / GLA_GATE_NORM, L)
        q_dec = (x[:, 0:128] * (GLA_HK ** -0.5) * jnp.exp(bc)).astype(bf16)
        pre[b] = (x, bc, q_dec, x[:, 128:256] * jnp.exp(-bc))
        st[b] = R["g_st"][b]

    def chain(j, b):
        x, bc, q_dec, k_dec = pre[b]
        rows = slice(j * L, (j + 1) * L)
        s_cat = _dot_nt(q_dec[rows], _stack_heads(k_dec[rows], k_masks))
        s_cat = jnp.where(causal, s_cat, 0.0)
        v = x[rows, 256:512]
        intra[j, b] = _dot(s_cat, _stack_heads(v, v_masks))
        b_last = bc[(j + 1) * L - 1:(j + 1) * L, :]
        k_end = x[rows, 128:256] * jnp.exp(b_last - bc[rows])
        kv[j, b] = jnp.where(blockdiag, _dot_tn(v, k_end), 0.0)
        dec[j, b] = jnp.exp(b_last)

    def link(j, b):
        q_dec = pre[b][2]
        outs[b][j] = intra[j, b] + _dot_nt(q_dec[j * L:(j + 1) * L], st[b])
        st[b] = st[b] * dec[j, b] + kv[j, b]

    def finish(b):
        R["g_st"][b] = st[b]
        o = jnp.concatenate(outs[b], axis=0)
        ms = _seg_dot(o * o, R["ones"][...]) * (1.0 / GLA_HV)
        o = o * lax.rsqrt(ms + GROUP_NORM_EPS) * R["g_nw"][...]
        R["o_gla"][b] = (o * jax.nn.silu(pre[b][0][:, 512:768])).astype(bf16)

    P_ = functools.partial
    chains = [(j, b) for j in range(nch) for b in range(nb)]
    return ([(350, P_(prep, b)) for b in range(nb)] + [(150, P_(chain, j, b)) for j, b in chains]
            + [(80, P_(link, j, b)) for j, b in chains] + [(300, P_(finish, b)) for b in range(nb)])


def _lru_stages(R, P, nb):
    tc = MIX_TILE
    w = W_GROUP
    row = lax.broadcasted_iota(jnp.int32, (tc, w), 0)
    au = {}

    def gates(b):
        x = P[b]["lru"]
        xb = x[:, 0:w]
        ext = R["l_ext"]
        ext[b, 8:, :] = xb
        xc = R["l_cb"][...] + R["l_cw"][CONV_W - 1:CONV_W, :] * xb
        for j in range(1, CONV_W):
            xc = xc + R["l_cw"][CONV_W - 1 - j:CONV_W - j, :] * ext[b, pl.ds(8 - j, tc), :]
        ext[b, 0:8, :] = xb[tc - 8:tc, :]
        ri = jax.nn.sigmoid(_dot(xc, R["l_wax"][...]) + R["l_bax"][...])
        log_a = -LRU_C * ri[:, 0:w] * jax.nn.softplus(-R["l_lam"][...])
        a = jnp.exp(log_a)
        au[b] = (a, jnp.sqrt(1.0 - a * a) * (ri[:, w:2 * w] * xc))

    def scan(b, shifts):
        a, u = au[b]
        for s in shifts:
            keep = row >= s
            u = u + a * jnp.where(keep, pltpu.roll(u, s, 0), 0.0)
            a = a * jnp.where(keep, pltpu.roll(a, s, 0), 1.0)
        au[b] = (a, u)

    def finish(b):
        a, u = au[b]
        h = u + a * R["l_h"][b]
        R["l_h"][b] = h[tc - 1:tc, :]
        R["o_lru"][b] = (h * jax.nn.gelu(P[b]["lru"][:, w:2 * w])).astype(bf16)

    shifts = [1 << i for i in range(tc.bit_length() - 1)]
    P_ = functools.partial
    out = []
    for b in range(nb):
        out += [(700, P_(gates, b))] + [(260, P_(scan, b, shifts[i:i + 2])) for i in range(0, len(shifts), 2)] \
            + [(250, P_(finish, b))]
    return out


def _rw_stages(R, P, nb, mix):
    L = RW_CHUNK
    tc = MIX_TILE
    nch = tc // L
    w = W_GROUP
    n4 = 4 * L
    row3 = lax.broadcasted_iota(jnp.int32, (tc, 3 * w), 0)
    rows_sm = lax.broadcasted_iota(jnp.int32, (tc, SM_W), 0)
    rr = lax.broadcasted_iota(jnp.int32, (n4, n4), 0)
    cc = lax.broadcasted_iota(jnp.int32, (n4, n4), 1)
    strict = (rr % L) > (cc % L)
    incl = (rr % L) >= (cc % L)
    eye = (rr == cc).astype(f32).astype(bf16)
    masks = _head_masks(L, w)
    w0, a0, k_k, k_a, r_k, gn_w, gn_b, v0 = (R["r_vec"][i:i + 1, :] for i in range(8))
    lvl = R["lvl"]
    pre, hs = {}, {}
    ch = {}
    ys = [[None] * nch for _ in range(nb)]
    keys = [(j, b) for j in range(nch) for b in range(nb)]

    def prep_a(b):
        x = P[b]["rw"]
        sm = P[b]["sm"]
        x_prev = jnp.where(row3 == 0, R["r_cx"][b], pltpu.roll(x, 1, 0))
        sm_prev = jnp.where(rows_sm == 0, R["r_cs"][b], pltpu.roll(sm, 1, 0))
        R["r_cx"][b] = x[tc - 1:tc, :]
        R["r_cs"][b] = sm[tc - 1:tc, :]
        x = x + (x_prev - x) * R["r_mu"][...]
        sm = sm + (sm_prev - sm) * R["r_mus"][...]
        r = x[:, 0:w]
        k = x[:, w:2 * w]
        v = x[:, 2 * w:3 * w]
        log_w = -RW_DECAY_SCALE * jax.nn.sigmoid(w0 + _dot(jnp.tanh(sm), R["r_w2"][...]))
        a = jax.nn.sigmoid(a0 + _dot(sm, R["r_a2"][...]))
        g = _dot(jax.nn.sigmoid(sm), R["r_g2"][...])
        if mix:
            lam = jax.nn.sigmoid(v0 + _dot(_dot(v, R["r_v1"][...]), R["r_v2"][...]))
            v = v + (R["vf"][b] - v) * lam
        else:
            R["o_v"][b] = v
        pre[b] = dict(r=r, k=k, v=v, g=g, a=a, log_w=log_w)

    def prep_b(b):
        p = pre[b]
        r, k, v, g, a, log_w = (p[n] for n in ("r", "k", "v", "g", "a", "log_w"))
        kk = k * k_k
        n2 = _seg_dot(kk * kk, R["ones"][...])
        kk = kk / jnp.maximum(jnp.sqrt(n2), 1e-12)
        k = k * (1.0 + (a - 1.0) * k_a)
        cum = _cumsum_rows(log_w, L)
        g_in = jnp.exp(cum)
        g_inv = jnp.exp(-cum)
        pre[b] = dict(r=r, k=k, v=v, g=g, g_in=g_in,
                      al=-kk * jnp.exp(cum - log_w),
                      be=kk * a * g_inv, kt=k * g_inv, rt=r * g_in)
        hs[b] = R["r_h"][b]

    def scores(j, b):
        p = pre[b]
        rows = slice(j * L, (j + 1) * L)
        g_last = p["g_in"][(j + 1) * L - 1:(j + 1) * L, :]
        d = dict()
        d["g_col"] = jnp.broadcast_to(jnp.broadcast_to(g_last, (8, w)).T[:, 0:1], (w, w))
        d["al"] = _stack_heads(p["al"][rows], masks)
        d["rt"] = _stack_heads(p["rt"][rows], masks)
        d["vst"] = _stack_heads(p["v"][rows], masks)
        be = _stack_heads(p["be"][rows], masks)
        kt = _stack_heads(p["kt"][rows], masks)
        d["be_end"] = _stack_heads(p["be"][rows] * g_last, masks)
        kt_end = _stack_heads(p["kt"][rows] * g_last, masks)
        aa = _dot_nt(jnp.concatenate([d["al"], d["rt"]], axis=0), jnp.concatenate([be, kt], axis=0))
        d["a_ab"] = jnp.where(strict, aa[0:n4, 0:n4], 0.0).astype(bf16)
        d["a_ak"] = jnp.where(strict, aa[0:n4, n4:2 * n4], 0.0).astype(bf16)
        d["a_rb"] = jnp.where(incl, aa[n4:2 * n4, 0:n4], 0.0).astype(bf16)
        d["a_rk"] = jnp.where(incl, aa[n4:2 * n4, n4:2 * n4], 0.0).astype(bf16)
        d["kv"] = _dot_tn(kt_end, d["vst"])
        ch[j, b] = d

    def level0():
        for key in keys:
            ch[key]["t"] = eye + ch[key]["a_ab"] * lvl[0]

    def level_a(lv, part):
        s = 1 << lv
        for key in part:
            t = ch[key]["t"]
            if s >= 16:
                t = jnp.concatenate([t[r:r + s] for r in range(s, n4, 2 * s)], axis=0)
            ch[key]["x1"] = _dot(t, ch[key]["a_ab"] * lvl[lv])

    def level_b(lv, part):
        s = 1 << lv
        for key in part:
            t = ch[key]["t"]
            x = _dot(ch[key]["x1"], t).astype(bf16)
            if s >= 16:
                pieces = []
                for i, r in enumerate(range(0, n4, 2 * s)):
                    pieces += [t[r:r + s], t[r + s:r + 2 * s] + x[i * s:(i + 1) * s]]
                ch[key]["t"] = jnp.concatenate(pieces, axis=0)
            else:
                ch[key]["t"] = t + x

    def solve(j, b):
        d = ch[j, b]
        ak = _dot(jnp.concatenate([d["a_ak"], d["a_rk"]], axis=0), d["vst"])
        pq = _dot(d["t"], jnp.concatenate([d["al"], ak[0:n4].astype(bf16)], axis=1))
        d["p_st"] = pq[:, 0:w].astype(bf16)
        d["q_st"] = pq[:, w:2 * w]
        d["y_k"] = ak[n4:2 * n4]

    def link(j, b):
        d = ch[j, b]
        h_bf = hs[b].astype(bf16)
        u = (_dot(d["p_st"], h_bf) + d["q_st"]).astype(bf16)
        hs[b] = hs[b] * d["g_col"] + _dot_tn(d["be_end"], u) + d["kv"]
        y_st = _dot(d["rt"], h_bf) + _dot(d["a_rb"], u) + d["y_k"]
        ys[b][j] = y_st[0:L] + y_st[L:2 * L] + y_st[2 * L:3 * L] + y_st[3 * L:4 * L]

    def finish(b):
        R["r_h"][b] = hs[b]
        p = pre[b]
        y = jnp.concatenate(ys[b], axis=0)
        mean = _seg_dot(y, R["ones"][...]) * (1.0 / RW_HS)
        dlt = y - mean
        var = _seg_dot(dlt * dlt, R["ones"][...]) * (1.0 / RW_HS)
        yn = dlt * lax.rsqrt(var + RW_GN_EPS) * gn_w + gn_b
        bonus = _seg_dot(p["r"] * p["k"] * r_k, R["ones"][...]) * p["v"]
        R["o_rw"][b] = ((yn + bonus) * p["g"]).astype(bf16)

    P_ = functools.partial
    n_lvl = RW_CHUNK.bit_length() - 1
    halves = (keys[:len(keys) // 2], keys[len(keys) // 2:])
    return ([st for b in range(nb) for st in ((900, P_(prep_a, b)), (900, P_(prep_b, b)))],
            [(450, P_(scores, j, b)) for j, b in keys] + [(150, level0)]
            + [(300, st) for lv in range(1, n_lvl) for st in
               (P_(level_a, lv, halves[0]), P_(level_a, lv, halves[1]), P_(level_b, lv, halves[0]), P_(level_b, lv, halves[1]))]
            + [(350, P_(solve, j, b)) for j, b in keys],
            [(300, P_(link, j, b)) for j, b in keys] + [(500, P_(finish, b)) for b in range(nb)])


def _ssd_stages(R, P, nb):
    L = SSD_CHUNK
    tc = MIX_TILE
    nch = tc // L
    w = W_GROUP
    cdim = 3 * w
    ll = lax.broadcasted_iota(jnp.int32, (L, L), 0)
    ss = lax.broadcasted_iota(jnp.int32, (L, L), 1)
    causal = ll >= ss
    lane_head = lax.broadcasted_iota(jnp.int32, (L, w), 1) // (w // SSD_HEADS)
    pre, hst = {}, {}
    ys = [[None] * nch for _ in range(nb)]

    def prep(b):
        x = P[b]["ssd"]
        xbc = x[:, w:4 * w]
        ext = R["s_ext"]
        ext[b, 8:, :] = xbc
        conv = R["s_cb"][...] + R["s_cw"][CONV_W - 1:CONV_W, :] * xbc
        for j in range(1, CONV_W):
            conv = conv + R["s_cw"][CONV_W - 1 - j:CONV_W - j, :] * ext[b, pl.ds(8 - j, tc), :]
        ext[b, 0:8, :] = xbc[tc - 8:tc, :]
        xbc = jax.nn.silu(conv)
        dt = jax.nn.softplus(P[b]["sm"] + R["s_dtb"][...])
        d_a = dt * (-jnp.exp(R["s_alog"][0:1, :]) * R["s_alog"][1:2, :])
        cs = _cumsum_rows(d_a, L)
        pre[b] = dict(xbc=xbc, cs=cs, dt_full=_expand_heads(dt, SM_DT, w), cs_full=_expand_heads(cs, SM_DT, w))
        hst[b] = R["s_h"][b]

    cur = {}

    def chunk_pre(j, b):
        p = pre[b]
        rows = slice(j * L, (j + 1) * L)
        xs = p["xbc"][rows, 0:w]
        cs = p["cs"][rows]
        cs_full = p["cs_full"][rows]
        cs_last = cs_full[L - 1:L, :]
        xdt = xs * p["dt_full"][rows]
        cur[b] = dict(xs=xs, bm=p["xbc"][rows, w:2 * w], cm=p["xbc"][rows, 2 * w:3 * w], cs=cs, cs_t=cs.T,
                      cs_full=cs_full, cs_last=cs_last, xdt=xdt, xw=xdt * jnp.exp(cs_last - cs_full),
                      y=jnp.zeros((L, w), f32), y_off=[], st=[])
        cur[b]["bm_t"] = cur[b]["bm"].T

    def group(b, grp):
        q = cur[b]
        lo, hi = grp * SSD_DSTATE, (grp + 1) * SSD_DSTATE
        cb_g = _dot_nt(q["cm"][:, lo:hi], q["bm"][:, lo:hi])
        for r in range(2):
            hd = grp * 2 + r
            seg = q["cs"][:, SM_DT + hd:SM_DT + hd + 1] - q["cs_t"][SM_DT + hd:SM_DT + hd + 1, :]
            m = cb_g * jnp.where(causal, jnp.exp(seg), 0.0)
            q["y"] = q["y"] + _dot(m, jnp.where(lane_head == hd, q["xdt"], 0.0))
        q["y_off"].append(_dot(q["cm"][:, lo:hi], hst[b][:, lo:hi]))
        q["st"].append(_dot(q["bm_t"][lo:hi, :], q["xw"][:, lo:hi]))

    def chunk_fin(j, b):
        q = cur[b]
        y = q["y"] + jnp.concatenate(q["y_off"], axis=1) * jnp.exp(q["cs_full"])
        hst[b] = hst[b] * jnp.exp(q["cs_last"]) + jnp.concatenate(q["st"], axis=1)
        ys[b][j] = y + R["s_dsk"][...] * q["xs"]

    def finish(b):
        R["s_h"][b] = hst[b]
        y = jnp.concatenate(ys[b], axis=0) * jax.nn.silu(P[b]["ssd"][:, 0:w])
        outs = []
        for grp in range(2):
            yg = y[:, grp * 128:(grp + 1) * 128]
            ms = jnp.mean(yg * yg, axis=-1, keepdims=True)
            outs.append(yg * lax.rsqrt(ms + GROUP_NORM_EPS))
        R["o_ssd"][b] = (jnp.concatenate(outs, axis=1) * R["s_nw"][...]).astype(bf16)

    P_ = functools.partial
    out = []
    for b in range(nb):
        out += [(1000, P_(prep, b))]
        for j in range(nch):
            out += [(200, P_(chunk_pre, j, b)), (350, P_(group, b, 0)), (350, P_(group, b, 1)), (150, P_(chunk_fin, j, b))]
        out += [(350, P_(finish, b))]
    return out


_MIX_IN = ("x", "nw", "w_in", "g_wup", "g_bias", "g_nw", "l_cw", "l_cb", "l_wax", "l_bax", "l_lam",
           "r_mu", "r_mus", "r_w2", "r_a2", "r_g2", "r_vec", "s_cw", "s_cb", "s_dtb", "s_alog", "s_dsk", "s_nw",
           "ones", "lvl")
_MIX_SCRATCH = ("g_st", "l_ext", "l_h", "r_cx", "r_cs", "r_h", "s_ext", "s_h")


def _mix_body(*refs, names, nb, mix):
    R = dict(zip(names, refs))
    c = pl.program_id(0)

    @pl.when(c == 0)
    def _():
        for name in ("g_st", "l_h", "r_cx", "r_cs", "r_h", "s_h"):
            R[name][...] = jnp.zeros_like(R[name])
        for b in range(nb):
            R["l_ext"][b, 0:8, :] = jnp.zeros((8, W_GROUP), f32)
            R["s_ext"][b, 0:8, :] = jnp.zeros((8, 3 * W_GROUP), f32)

    P = [dict() for _ in range(nb)]
    xn = {}

    def norm(b):
        xn[b] = _rms_norm(R["x"][b], R["nw"][...]).astype(bf16)

    def project(b, seg):
        lo, hi = _SEGS[seg]
        P[b][seg] = jnp.dot(xn[b], R["w_in"][:, lo:hi], preferred_element_type=f32)

    P_ = functools.partial
    batches = range(nb)
    rw_prep, rw_chain, rw_tail = _rw_stages(R, P, nb, mix)
    head = [P_(norm, b) for b in batches] + [P_(project, b, seg) for b in batches for seg in ("rw", "sm")]
    rest_proj = [((_SEGS[seg][1] - _SEGS[seg][0]), P_(project, b, seg)) for seg in ("ssd", "lru", "gla") for b in batches]
    others = _ssd_stages(R, P, nb) + _lru_stages(R, P, nb) + _gla_stages(R, P, nb)
    for stage in head + _merge(rw_prep, rest_proj) + _merge(rw_chain + rw_tail, others):
        stage()


def _mixers(x3, nw, w_in_p, layer, v_first, prm):
    nb, s, d = x3.shape
    tc = MIX_TILE
    w = W_GROUP
    mix = v_first is not None
    blk = lambda width: pl.BlockSpec((nb, tc, width), lambda c: (0, c, 0))
    full = lambda a: pl.BlockSpec(a.shape, lambda c: (0,) * a.ndim)
    of_layer = lambda a: pl.BlockSpec((None,) + a.shape[1:], lambda c: (layer,) + (0,) * (a.ndim - 1))
    names = list(_MIX_IN) + (["r_v1", "r_v2", "vf"] if mix else [])
    arrays = dict(prm, x=x3, nw=nw, w_in=w_in_p, vf=v_first)
    specs = dict(x=blk(d), vf=blk(w), ones=full(prm["ones"]), lvl=full(prm["lvl"]),
                 w_in=pl.BlockSpec((None, d, w_in_p.shape[-1]), lambda c: (layer, 0, 0),
                                   pipeline_mode=pl.Buffered(1)))
    ins = [arrays[n] for n in names]
    in_specs = [specs[n] if n in specs else of_layer(arrays[n]) for n in names]
    out_names = ["o_gla", "o_lru", "o_rw", "o_ssd"] + ([] if mix else ["o_v"])
    out_shape = [jax.ShapeDtypeStruct((nb, s, w), bf16)] * 4 + ([] if mix else [jax.ShapeDtypeStruct((nb, s, w), f32)])
    scratch = [pltpu.VMEM((nb, w, GLA_DK), f32), pltpu.VMEM((nb, tc + 8, w), f32), pltpu.VMEM((nb, 1, w), f32),
               pltpu.VMEM((nb, 1, 3 * w), f32), pltpu.VMEM((nb, 1, SM_W), f32), pltpu.VMEM((nb, w, w), f32),
               pltpu.VMEM((nb, tc + 8, 3 * w), f32), pltpu.VMEM((nb, SSD_DSTATE, w), f32)]
    outs = pl.pallas_call(
        functools.partial(_mix_body, names=tuple(names + out_names) + _MIX_SCRATCH, nb=nb, mix=mix),
        grid=(s // tc,),
        in_specs=in_specs,
        out_specs=[blk(w)] * len(out_names),
        out_shape=out_shape,
        scratch_shapes=scratch,
        compiler_params=pltpu.CompilerParams(
            dimension_semantics=("arbitrary",), vmem_limit_bytes=VMEM_LIMIT),
        name="mixers",
    )(*ins)
    return outs[:4], (v_first if mix else outs[4])


def _level_masks():
    n = 4 * RW_CHUNK
    t = np.arange(n)[:, None]
    u = np.arange(n)[None, :]
    out = []
    s = 1
    while s < RW_CHUNK:
        out.append(((t // (2 * s)) == (u // (2 * s))) & ((t % (2 * s)) >= s) & ((u % (2 * s)) < s))
        s *= 2
    return jnp.asarray(np.stack(out).astype(np.float32), dtype=bf16)


def _ones_blocks(seg):
    i = np.arange(W_GROUP)
    return jnp.asarray((i[:, None] // seg == i[None, :] // seg).astype(np.float32), dtype=bf16)


def _rows(a):
    return a[:, None, :]


def _pad_rows(m, row0):
    return jnp.pad(m, ((0, 0), (row0, SM_W - row0 - m.shape[1]), (0, 0)))


def _lane_rows(v, lane0):
    return jnp.pad(v, ((0, 0), (lane0, SM_W - lane0 - v.shape[1])))[:, None, :]


def _block_diag(w):
    nl, n, k, j = w.shape
    return jnp.einsum("lnkj,nm->lnkmj", w, jnp.eye(n, dtype=w.dtype)).reshape(nl, n * k, n * j)


def kernel(x, ffn1_norm, ffn1_w_gate, ffn1_w_up, ffn1_w_down, mix_norm, w_in, w_out, gla_alpha_up, gla_alpha_bias, gla_norm, lru_conv_w, lru_conv_b, lru_w_a, lru_b_a, lru_w_x, lru_b_x, lru_lambda, rw_mu, rw_w0, rw_w2, rw_a0, rw_a2, rw_g2, rw_v0, rw_v1, rw_v2, rw_k_k, rw_k_a, rw_r_k, rw_gn_w, rw_gn_b, ssd_conv_w, ssd_conv_b, ssd_dt_bias, ssd_a_log, ssd_d, ssd_norm, ffn2_norm, ffn2_w_gate, ffn2_w_up, ffn2_w_down, final_norm):
    nb, s, d = x.shape
    t = nb * s
    depth = w_in.shape[0]
    tm = min(FFN_TILE, t)

    wg1, wu1, wd1 = (a.astype(bf16) for a in (ffn1_w_gate, ffn1_w_up, ffn1_w_down))
    wg2, wu2, wd2 = (a.astype(bf16) for a in (ffn2_w_gate, ffn2_w_up, ffn2_w_down))
    w_out_b = w_out.astype(bf16)
    w_in_b = w_in.astype(bf16)
    w_in_p = jnp.concatenate([
        w_in_b[:, :, _GLA0:_GLA0 + 768], w_in_b[:, :, _LRU0:_LRU0 + 512], w_in_b[:, :, _RW0:_RW0 + 768],
        w_in_b[:, :, _SSD0:_SSD0 + 1024],
        w_in_b[:, :, _GLA0 + 768:_GLA0 + 784], w_in_b[:, :, _RW0 + 768:_RW0 + 832],
        w_in_b[:, :, _SSD0 + 1024:_SSD0 + 1028], jnp.zeros((depth, d, SM_W - 84), bf16)], axis=-1)
    rank_v = rw_v1.shape[-1]
    prm = dict(
        g_wup=_pad_rows(gla_alpha_up, SM_GLA), g_bias=_rows(gla_alpha_bias),
        g_nw=_rows(jnp.tile(gla_norm, (1, GLA_HEADS))),
        l_cw=lru_conv_w, l_cb=_rows(lru_conv_b),
        l_wax=jnp.concatenate([_block_diag(lru_w_a), _block_diag(lru_w_x)], axis=-1).astype(bf16),
        l_bax=_rows(jnp.concatenate([lru_b_a, lru_b_x], axis=-1)), l_lam=_rows(lru_lambda),
        r_mu=_rows(rw_mu[:, 0:768]), r_mus=_lane_rows(rw_mu[:, 768:832], SM_RW),
        r_w2=_pad_rows(rw_w2, SM_RW), r_a2=_pad_rows(rw_a2, SM_RW + 16), r_g2=_pad_rows(rw_g2, SM_RW + 32),
        r_vec=jnp.stack([rw_w0, rw_a0, rw_k_k, rw_k_a, rw_r_k.reshape(depth, -1), rw_gn_w, rw_gn_b,
                         jnp.pad(rw_v0, ((1, 0), (0, 0)))], axis=1),
        r_v1=jnp.pad(rw_v1, ((1, 0), (0, 0), (0, 128 - rank_v))),
        r_v2=jnp.pad(rw_v2, ((1, 0), (0, 128 - rank_v), (0, 0))),
        s_cw=ssd_conv_w, s_cb=_rows(ssd_conv_b), s_dtb=_lane_rows(ssd_dt_bias, SM_DT),
        s_alog=jnp.concatenate([_lane_rows(ssd_a_log, SM_DT), _lane_rows(jnp.ones_like(ssd_a_log), SM_DT)], axis=1),
        s_dsk=_rows(jnp.repeat(ssd_d, W_GROUP // SSD_HEADS, axis=-1)), s_nw=_rows(ssd_norm),
        ones=_ones_blocks(64), lvl=_level_masks())
    n1, n2, nm = _rows(ffn1_norm), _rows(ffn2_norm), _rows(mix_norm)

    x2 = x.reshape(t, d)
    v_first = None
    for l in range(depth):
        x2 = _ffn(x2, n1, wg1, wu1, wd1, l, tm=tm)
        ys, v_first = _mixers(x2.reshape(nb, s, d), nm, w_in_p, l, v_first, prm)
        ys = [a.reshape(t, W_GROUP) for a in ys]
        x2 = _ffn(x2, n2, wg2, wu2, wd2, l, tm=tm, ys=ys, w_out=w_out_b,
                  final_w=final_norm[None, :] if l == depth - 1 else None)
    return x2.reshape(nb, s, d)
```
